```python
import jax
import jax.numpy as jnp
from jax import lax
import numpy as np

D_MODEL = 2048
BATCH = 8
SEQ = 2048
DEPTH = 2
DEC_BATCH = 128
DEC_SEQ = 4
PAST_LEN = 2048
PAGE_SIZE = 128

HEAD_DIM = 64
D_A = D_MODEL // 2
H_A = D_A // HEAD_DIM
D_B = D_MODEL // 2
H_B = D_B // HEAD_DIM
W_LORA = 64
A_LORA = 64
G_LORA = 160
RWKV_PROJ = 3 * D_A + W_LORA + A_LORA + G_LORA
RWKV_SPLITS = (D_A, 2 * D_A, 3 * D_A, 3 * D_A + W_LORA, 3 * D_A + W_LORA + A_LORA)
SB_PROJ = 3 * D_B
GATE_PROJ = 2 * D_MODEL
PROJ_TOTAL = RWKV_PROJ + SB_PROJ + GATE_PROJ
Q_BLOCK = 128
N_PEER_HEADS = 8
N_KEYS = 128
N_EXPERTS = N_KEYS * N_KEYS
D_KEY = 256
TOPK = 16
PEER_BLOCK = 128
ALPHA = (2 * DEPTH) ** 0.25
BETA = (8 * DEPTH) ** -0.25
LN_EPS = 1e-5
GN_EPS = 64e-5

kernel_name = 'hybrid_rwkv7_stickbreak_peer_step'


def _standardize(x, eps):
    x32 = x.astype(jnp.float32)
    mu = jnp.mean(x32, axis=-1, keepdims=True)
    var = jnp.mean(jnp.square(x32 - mu), axis=-1, keepdims=True)
    return (x32 - mu) * lax.rsqrt(var + eps)


def layer_norm(x, g, b):
    return (_standardize(x, LN_EPS) * g + b).astype(x.dtype)


def ada_modulation(c, w_ada, b_ada):
    mod = jax.nn.silu(c) @ w_ada + b_ada
    return jnp.split(mod[:, None, :], 6, axis=-1)


def modulate(x, shift, scale):
    return (_standardize(x, LN_EPS) * (1 + scale) + shift).astype(x.dtype)


def wkv7_scan(r, w, k, v, a, b, s0):
    def step(s, inp):
        r_t, w_t, k_t, v_t, a_t, b_t = inp
        sa = jnp.einsum('bhvk,bhk->bhv', s, a_t)
        s = s * w_t[:, :, None, :] + sa[..., None] * b_t[:, :, None, :] + v_t[..., None] * k_t[:, :, None, :]
        return s, jnp.einsum('bhvk,bhk->bhv', s, r_t)
    xs = tuple(jnp.moveaxis(t, 1, 0) for t in (r, w, k, v, a, b))
    s_final, ys = lax.scan(step, s0, xs)
    return jnp.moveaxis(ys, 0, 1), s_final


def rwkv7_branch(p, prev_row, s0, mu, decay_bias, decay_up, aaa_bias, aaa_up, og_up, k_k, k_a, r_k, lnx_g, lnx_b):
    B, T, _ = p.shape
    p_prev = jnp.concatenate([prev_row[:, None, :].astype(p.dtype), p[:, :-1]], axis=1)
    pm = p + (p_prev - p) * mu
    r, k, v, xw, xa, xg = jnp.split(pm, RWKV_SPLITS, axis=-1)
    w_pre = (decay_bias + jnp.tanh(xw) @ decay_up).astype(jnp.float32)
    decay = jnp.exp(-jnp.exp(-jax.nn.softplus(-w_pre) - 0.5))
    a = jax.nn.sigmoid((aaa_bias + xa @ aaa_up).astype(jnp.float32))
    og = jax.nn.sigmoid(xg) @ og_up

    def to_heads(t):
        return t.astype(jnp.float32).reshape(B, T, H_A, HEAD_DIM)

    r, k, v, decay, a = to_heads(r), to_heads(k), to_heads(v), to_heads(decay), to_heads(a)
    kk = k * k_k.reshape(H_A, HEAD_DIM).astype(jnp.float32)
    kk = kk * lax.rsqrt(jnp.sum(kk * kk, axis=-1, keepdims=True) + 1e-12)
    k = k * (1 + (a - 1) * k_a.reshape(H_A, HEAD_DIM).astype(jnp.float32))
    y, s_final = wkv7_scan(r, decay, k, v, -kk, kk * a, s0.astype(jnp.float32))
    y = _standardize(y, GN_EPS) * lnx_g.reshape(H_A, HEAD_DIM) + lnx_b.reshape(H_A, HEAD_DIM)
    y = y + jnp.sum(r * k * r_k.astype(jnp.float32), axis=-1, keepdims=True) * v
    y = y.reshape(B, T, D_A) * og
    return y.astype(p.dtype), s_final, p[:, -1]


def stick_breaking_block(q, k, v, q_pos, k_pos, bias):
    z = jnp.einsum('bqhn,bshn->bhqs', q, k).astype(jnp.float32) * (HEAD_DIM ** -0.5)
    z = z + bias.astype(jnp.float32)[None, :, None, None]
    visible = k_pos[None, :] < q_pos[:, None]
    log_keep = jnp.where(visible, jax.nn.log_sigmoid(-z), 0.0)
    later_keep = lax.cumsum(log_keep, axis=3, reverse=True) - log_keep
    weights = jnp.where(visible, jnp.exp(jax.nn.log_sigmoid(z) + later_keep), 0.0)
    return jnp.einsum('bhqs,bshn->bqhn', weights.astype(v.dtype), v)


def stick_breaking_attention(q, k, v, q_start, bias):
    B, T, H, N = q.shape
    S = k.shape[1]
    blk = min(Q_BLOCK, T)
    nb = T // blk
    q_blocks = jnp.moveaxis(q.reshape(B, nb, blk, H, N), 1, 0)
    q_pos = (q_start + jnp.arange(T, dtype=jnp.int32)).reshape(nb, blk)
    k_pos = jnp.arange(S, dtype=jnp.int32)
    out = lax.map(lambda qp: stick_breaking_block(qp[0], k, v, qp[1], k_pos, bias), (q_blocks, q_pos))
    return jnp.moveaxis(out, 0, 1).reshape(B, T, H, N)


def peer_ffn(h, wq, sub_keys, u_tab, v_tab):
    B, T, D = h.shape
    n = B * T
    n_pad = -(-n // PEER_BLOCK) * PEER_BLOCK
    blocks = jnp.pad(h.reshape(n, D), ((0, n_pad - n), (0, 0))).reshape(n_pad // PEER_BLOCK, PEER_BLOCK, D)

    def one_block(x):
        q = (x @ wq).reshape(PEER_BLOCK, N_PEER_HEADS, 2, D_KEY // 2)
        s = jnp.einsum('nhpk,hpek->nhpe', q, sub_keys).astype(jnp.float32)
        top_s, top_i = lax.top_k(s, TOPK)
        cand_s = (top_s[:, :, 0, :, None] + top_s[:, :, 1, None, :]).reshape(PEER_BLOCK, N_PEER_HEADS, TOPK * TOPK)
        cand_i = (top_i[:, :, 0, :, None] * N_KEYS + top_i[:, :, 1, None, :]).reshape(PEER_BLOCK, N_PEER_HEADS, TOPK * TOPK)
        best_s, best_j = lax.top_k(cand_s, TOPK)
        expert = jnp.take_along_axis(cand_i, best_j, axis=-1)
        gate = jax.nn.softmax(best_s, axis=-1)
        act = jax.nn.gelu(jnp.einsum('nhkd,nd->nhk', u_tab[expert], x))
        return jnp.einsum('nhk,nhkd->nd', (gate * act).astype(x.dtype), v_tab[expert])

    out = lax.map(one_block, blocks)
    return out.reshape(n_pad, D)[:n].reshape(B, T, D)


def run_trunk(x, c, shift0, wkv0, cache_k, cache_v, page_table, prm):
    B, T, _ = x.shape
    k_rows, v_rows, wkv_rows, shift_rows = [], [], [], []
    for l in range(DEPTH):
        sh_m, sc_m, g_m, sh_f, sc_f, g_f = ada_modulation(c, prm['w_ada'][l], prm['b_ada'][l])
        h = modulate(x, sh_m, sc_m)
        proj = h @ prm['w_in'][l]
        p_rwkv, p_sb, p_gate = jnp.split(proj, [RWKV_PROJ, RWKV_PROJ + SB_PROJ], axis=-1)
        y_a, wkv_l, shift_l = rwkv7_branch(
            p_rwkv, shift0[l], wkv0[l], prm['mu_shift'][l], prm['decay_bias'][l], prm['decay_up'][l],
            prm['aaa_bias'][l], prm['aaa_up'][l], prm['og_up'][l], prm['k_k'][l], prm['k_a'][l],
            prm['r_k'][l], prm['lnx_g'][l], prm['lnx_b'][l])
        q, k, v = [t.reshape(B, T, H_B, HEAD_DIM) for t in jnp.split(p_sb, 3, axis=-1)]
        if cache_k is None:
            k_all, v_all, q_start = k, v, 0
        else:
            n_past = page_table.shape[1] * PAGE_SIZE
            k_past = cache_k[l][page_table].reshape(B, n_past, H_B, HEAD_DIM).astype(k.dtype)
            v_past = cache_v[l][page_table].reshape(B, n_past, H_B, HEAD_DIM).astype(v.dtype)
            k_all = jnp.concatenate([k_past, k], axis=1)
            v_all = jnp.concatenate([v_past, v], axis=1)
            q_start = n_past
        y_b = stick_breaking_attention(q, k_all, v_all, q_start, prm['sb_bias'][l]).reshape(B, T, D_B)
        gate_a, gate_b = jnp.split(jax.nn.sigmoid(p_gate), 2, axis=-1)
        merged = gate_a * (y_a @ prm['w_branch_a'][l]) + gate_b * (y_b @ prm['w_branch_b'][l])
        x = layer_norm(ALPHA * x + g_m * (merged @ prm['w_o'][l]), prm['ln1_g'][l], prm['ln1_b'][l])
        h = modulate(x, sh_f, sc_f)
        ffn = peer_ffn(h, prm['peer_wq'][l], prm['peer_keys'][l], prm['peer_u'][l], prm['peer_v'][l])
        x = layer_norm(ALPHA * x + g_f * ffn, prm['ln2_g'][l], prm['ln2_b'][l])
        k_rows.append(k)
        v_rows.append(v)
        wkv_rows.append(wkv_l)
        shift_rows.append(shift_l)
    return x, jnp.stack(k_rows), jnp.stack(v_rows), jnp.stack(wkv_rows), jnp.stack(shift_rows)


def setup_inputs(seed: int = 0) -> dict:
    key = jax.random.key(seed)
    ks = iter(jax.random.split(key, 48))

    def nrm(shape, scale):
        return scale * jax.random.normal(next(ks), shape, jnp.float32)

    def unif(shape, lo, hi):
        return jax.random.uniform(next(ks), shape, jnp.float32, lo, hi)

    n_pages = PAST_LEN // PAGE_SIZE
    n_used = DEC_BATCH * n_pages
    n_phys = n_used + (n_used + 3) // 4
    page_table = jax.random.permutation(next(ks), n_phys)[:n_used].reshape(DEC_BATCH, n_pages).astype(jnp.int32)
    return {
        'x_prompt': nrm((BATCH, SEQ, D_MODEL), 1.0),
        'x_sample': nrm((DEC_BATCH, DEC_SEQ, D_MODEL), 1.0),
        'cache_k': nrm((DEPTH, n_phys, PAGE_SIZE, H_B, HEAD_DIM), 1.0),
        'cache_v': nrm((DEPTH, n_phys, PAGE_SIZE, H_B, HEAD_DIM), 1.0),
        'state_wkv': nrm((DEPTH, DEC_BATCH, H_A, HEAD_DIM, HEAD_DIM), 0.5),
        'state_shift': nrm((DEPTH, DEC_BATCH, RWKV_PROJ), 1.0),
        'page_table': page_table,
        'c_prompt': nrm((BATCH, D_MODEL), 1.0),
        'c_sample': nrm((DEC_BATCH, D_MODEL), 1.0),
        'w_ada': nrm((DEPTH, D_MODEL, 6 * D_MODEL), 0.5 * D_MODEL ** -0.5),
        'b_ada': nrm((DEPTH, 6 * D_MODEL), 0.01),
        'w_in': nrm((DEPTH, D_MODEL, PROJ_TOTAL), D_MODEL ** -0.5),
        'mu_shift': unif((DEPTH, RWKV_PROJ), 0.0, 1.0),
        'decay_bias': unif((DEPTH, D_A), -5.0, -0.5),
        'decay_up': nrm((DEPTH, W_LORA, D_A), 0.1 * W_LORA ** -0.5),
        'aaa_bias': nrm((DEPTH, D_A), 0.1),
        'aaa_up': nrm((DEPTH, A_LORA, D_A), A_LORA ** -0.5),
        'og_up': nrm((DEPTH, G_LORA, D_A), G_LORA ** -0.5),
        'k_k': 0.85 + nrm((DEPTH, D_A), 0.05),
        'k_a': 1.0 + nrm((DEPTH, D_A), 0.05),
        'r_k': nrm((DEPTH, H_A, HEAD_DIM), 0.1),
        'lnx_g': 1.0 + nrm((DEPTH, D_A), 0.05),
        'lnx_b': nrm((DEPTH, D_A), 0.01),
        'sb_bias': unif((DEPTH, H_B), -8.0, -5.0),
        'w_branch_a': nrm((DEPTH, D_A, D_MODEL), D_A ** -0.5),
        'w_branch_b': nrm((DEPTH, D_B, D_MODEL), D_B ** -0.5),
        'w_o': nrm((DEPTH, D_MODEL, D_MODEL), BETA * D_MODEL ** -0.5),
        'ln1_g': 1.0 + nrm((DEPTH, D_MODEL), 0.05),
        'ln1_b': nrm((DEPTH, D_MODEL), 0.01),
        'peer_wq': nrm((DEPTH, D_MODEL, N_PEER_HEADS * D_KEY), D_MODEL ** -0.5),
        'peer_keys': nrm((DEPTH, N_PEER_HEADS, 2, N_KEYS, D_KEY // 2), (D_KEY // 2) ** -0.5),
        'peer_u': nrm((DEPTH, N_EXPERTS, D_MODEL), D_MODEL ** -0.5),
        'peer_v': nrm((DEPTH, N_EXPERTS, D_MODEL), BETA * N_PEER_HEADS ** -0.5),
        'ln2_g': 1.0 + nrm((DEPTH, D_MODEL), 0.05),
        'ln2_b': nrm((DEPTH, D_MODEL), 0.01),
    }


def reference(x_prompt, x_sample, cache_k, cache_v, state_wkv, state_shift, page_table, c_prompt, c_sample,
              w_ada, b_ada, w_in, mu_shift, decay_bias, decay_up, aaa_bias, aaa_up, og_up, k_k, k_a, r_k,
              lnx_g, lnx_b, sb_bias, w_branch_a, w_branch_b, w_o, ln1_g, ln1_b, peer_wq, peer_keys, peer_u, peer_v,
              ln2_g, ln2_b):
    prm = {
        'w_ada': w_ada, 'b_ada': b_ada, 'w_in': w_in, 'mu_shift': mu_shift,
        'decay_bias': decay_bias, 'decay_up': decay_up, 'aaa_bias': aaa_bias, 'aaa_up': aaa_up,
        'og_up': og_up, 'k_k': k_k, 'k_a': k_a, 'r_k': r_k, 'lnx_g': lnx_g, 'lnx_b': lnx_b,
        'sb_bias': sb_bias,
        'w_branch_a': w_branch_a, 'w_branch_b': w_branch_b, 'w_o': w_o, 'ln1_g': ln1_g, 'ln1_b': ln1_b,
        'peer_wq': peer_wq, 'peer_keys': peer_keys, 'peer_u': peer_u, 'peer_v': peer_v,
        'ln2_g': ln2_g, 'ln2_b': ln2_b,
    }
    b_p = x_prompt.shape[0]
    shift0_p = jnp.zeros((DEPTH, b_p, RWKV_PROJ), x_prompt.dtype)
    wkv0_p = jnp.zeros((DEPTH, b_p, H_A, HEAD_DIM, HEAD_DIM), jnp.float32)
    y_prompt, k_prompt, v_prompt, wkv_prompt, shift_prompt = run_trunk(
        x_prompt, c_prompt, shift0_p, wkv0_p, None, None, None, prm)
    y_sample, k_sample, v_sample, wkv_sample, shift_sample = run_trunk(
        x_sample, c_sample, state_shift, state_wkv, cache_k, cache_v, page_table, prm)
    return (y_prompt, y_sample, k_prompt, v_prompt, wkv_prompt, shift_prompt,
            k_sample, v_sample, wkv_sample, shift_sample)
```

```python
import functools
import math

import jax
import jax.numpy as jnp
from jax import lax
from jax.experimental import pallas as pl
from jax.experimental.pallas import tpu as pltpu

F32 = jnp.float32
BF16 = jnp.bfloat16
LN_EPS = 1e-5
GN_EPS = 64e-5
KK_EPS = 1e-12
TOPK = 16
LANES = 128
SUBLANES = 8
VMEM_LIMIT = 56 * 1024 * 1024


def _cparams(*sem):
    return pltpu.CompilerParams(dimension_semantics=sem, vmem_limit_bytes=VMEM_LIMIT)


def _tile(n, pref):
    if n <= pref:
        return n
    t = (pref // LANES) * LANES
    while t > LANES and n % t:
        t -= LANES
    assert n % t == 0, (n, pref)
    return t


def _std(x, eps):
    mu = jnp.mean(x, axis=-1, keepdims=True)
    xc = x - mu
    var = jnp.mean(xc * xc, axis=-1, keepdims=True)
    return xc * lax.rsqrt(var + eps)


def _sigmoid(x):
    return 1.0 / (1.0 + jnp.exp(-x))


def _softplus(x):
    return jnp.maximum(x, 0.0) + jnp.log(1.0 + jnp.exp(-jnp.abs(x)))


def _split_dot(x, m, *, left=False):
    hi = x.astype(BF16)
    lo = (x - hi.astype(F32)).astype(BF16)
    if left:
        return (jnp.dot(m, hi, preferred_element_type=F32) + jnp.dot(m, lo, preferred_element_type=F32))
    return (jnp.dot(hi, m, preferred_element_type=F32) + jnp.dot(lo, m, preferred_element_type=F32))


def _head_sum(x, seg):
    outs = [_split_dot(x[:, c * LANES:(c + 1) * LANES], seg) for c in range(x.shape[1] // LANES)]
    return outs[0] if len(outs) == 1 else jnp.concatenate(outs, axis=1)


def _dot_nt(a, b):
    return lax.dot_general(a, b, (((1,), (1,)), ((), ())), preferred_element_type=F32)


def _dot_tn(a, b):
    return lax.dot_general(a, b, (((0,), (0,)), ((), ())), preferred_element_type=F32)


def _ada_kernel(c_ref, w_ref, b_ref, o_ref):
    c = c_ref[...]
    s = (c * _sigmoid(c)).astype(BF16)
    o_ref[...] = jnp.dot(s, w_ref[...], preferred_element_type=F32) + b_ref[...]


def _ada(c_all, w_bf, b_ada):
    depth, d, d6 = w_bf.shape
    nb = c_all.shape[0]
    tn = _tile(d6, 1024)
    return pl.pallas_call(
        _ada_kernel, grid=(depth, d6 // tn),
        in_specs=[pl.BlockSpec((nb, d), lambda l, j: (0, 0)),
                  pl.BlockSpec((None, d, tn), lambda l, j: (l, 0, j)),
                  pl.BlockSpec((None, 1, tn), lambda l, j: (l, 0, j))],
        out_specs=pl.BlockSpec((None, nb, tn), lambda l, j: (l, 0, j)),
        out_shape=jax.ShapeDtypeStruct((depth, nb, d6), F32),
        compiler_params=_cparams("arbitrary", "arbitrary"), name="ada",
    )(c_all, w_bf, b_ada.reshape(depth, 1, d6))


def _mod_spec(grp, which, tm):
    d = grp["d"]
    if grp["per_row"]:
        return pl.BlockSpec((tm, d), lambda i, *_: (i, which))
    t = grp["t"]
    return pl.BlockSpec((None, None, 1, d), lambda i, *_: ((i * tm) // t, which, 0, 0))


def _proj_kernel(x_ref, sh_ref, sc_ref, w_ref, o_ref, h_scr):
    @pl.when(pl.program_id(1) == 0)
    def _():
        h = _std(x_ref[...], LN_EPS) * (1.0 + sc_ref[...]) + sh_ref[...]
        h_scr[...] = h.astype(BF16)
    o_ref[...] = jnp.dot(h_scr[...], w_ref[...], preferred_element_type=F32)


def _proj(x, grp, mod, w_bf, tn):
    n, d = x.shape
    p = w_bf.shape[1]
    tm = grp["tm"]
    return pl.pallas_call(
        _proj_kernel, grid=(n // tm, p // tn),
        in_specs=[pl.BlockSpec((tm, d), lambda i, j: (i, 0)),
                  _mod_spec(grp, 0, tm), _mod_spec(grp, 1, tm),
                  pl.BlockSpec((d, tn), lambda i, j: (0, j))],
        out_specs=pl.BlockSpec((tm, tn), lambda i, j: (i, j)),
        out_shape=jax.ShapeDtypeStruct((n, p), F32),
        scratch_shapes=[pltpu.VMEM((tm, d), BF16)],
        compiler_params=_cparams("arbitrary", "arbitrary"), name="proj",
    )(x, mod, mod, w_bf)


def _rwkv_prep_kernel(t_seq, tm, da, per_row, pm_ref, pl_ref, prevm_ref, prevl_ref, mum_ref, mul_ref,
                      dbias_ref, abias_ref, wd_ref, wa_ref, wg_ref, kk_ref, ka_ref, seg_ref,
                      r_o, w_o, k_o, v_o, a_o, b_o, og_o, carm, carl):
    i = pl.program_id(0)
    row = lax.broadcasted_iota(jnp.int32, (tm, 1), 0)

    def shifted(p, prev_ref, car):
        rolled = pltpu.roll(p, 1, 0)
        if per_row:
            return jnp.where(lax.rem(row, t_seq) == 0, prev_ref[...], rolled)
        first = jnp.where(lax.rem(i, t_seq // tm) == 0, prev_ref[...], car[...])
        car[...] = p[tm - 1:tm, :]
        return jnp.where(row == 0, first, rolled)

    p = pm_ref[...]
    pm = p + (shifted(p, prevm_ref, carm) - p) * mum_ref[...]
    q = pl_ref[...]
    lo = q + (shifted(q, prevl_ref, carl) - q) * mul_ref[...]
    r = pm[:, :da]
    k = pm[:, da:2 * da]
    v = pm[:, 2 * da:3 * da]
    w_pre = dbias_ref[...] + jnp.dot(jnp.tanh(lo).astype(BF16), wd_ref[...], preferred_element_type=F32)
    decay = jnp.exp(-math.exp(-0.5) * _sigmoid(w_pre))
    a = _sigmoid(abias_ref[...] + jnp.dot(lo.astype(BF16), wa_ref[...], preferred_element_type=F32))
    og = jnp.dot(_sigmoid(lo).astype(BF16), wg_ref[...], preferred_element_type=F32)
    kk = k * kk_ref[...]
    kk = kk * lax.rsqrt(_head_sum(kk * kk, seg_ref[...]) + KK_EPS)
    r_o[...] = r
    w_o[...] = decay
    k_o[...] = k * (1.0 + (a - 1.0) * ka_ref[...])
    v_o[...] = v
    a_o[...] = -kk
    b_o[...] = kk * a
    og_o[...] = og


def _rwkv_prep(proj_main, proj_lora, grp, prev_main, prev_lora, prm, seg):
    n = proj_main.shape[0]
    da, lp = prm["da"], prm["lp"]
    tm = grp["tm_prep"]
    per_row = grp["t"] < tm
    row_spec = lambda w: pl.BlockSpec((tm, w), lambda i: (i, 0))
    vec_spec = lambda w: pl.BlockSpec((1, w), lambda i: (0, 0))
    if per_row:
        prev_specs = [row_spec(3 * da), row_spec(lp)]
    else:
        t = grp["t"]
        prev_specs = [pl.BlockSpec((None, 1, 3 * da), lambda i: ((i * tm) // t, 0, 0)),
                      pl.BlockSpec((None, 1, lp), lambda i: ((i * tm) // t, 0, 0))]
    out = jax.ShapeDtypeStruct((n, da), F32)
    return pl.pallas_call(
        functools.partial(_rwkv_prep_kernel, grp["t"], tm, da, per_row), grid=(n // tm,),
        in_specs=[row_spec(3 * da), row_spec(lp)] + prev_specs + [
            vec_spec(3 * da), vec_spec(lp), vec_spec(da), vec_spec(da),
            pl.BlockSpec((lp, da), lambda i: (0, 0)), pl.BlockSpec((lp, da), lambda i: (0, 0)),
            pl.BlockSpec((lp, da), lambda i: (0, 0)), vec_spec(da), vec_spec(da),
            pl.BlockSpec((LANES, LANES), lambda i: (0, 0))],
        out_specs=[row_spec(da)] * 7, out_shape=[out] * 7,
        scratch_shapes=[pltpu.VMEM((1, 3 * da), F32), pltpu.VMEM((1, lp), F32)],
        compiler_params=_cparams("arbitrary"), name="rwkv_prep",
    )(proj_main, proj_lora, prev_main, prev_lora, prm["mu_main"], prm["mu_lora"], prm["decay_bias"],
      prm["aaa_bias"], prm["wd"], prm["wa"], prm["wg"], prm["k_k"], prm["k_a"], seg)


def _scan_kernel(tt, n_pairs, r_ref, w_ref, k_ref, v_ref, a_ref, b_ref, s0_ref, seg_ref, dm_ref,
                 y_ref, sT_ref, s_scr):
    tc = pl.program_id(1)

    @pl.when(tc == 0)
    def _():
        s_scr[...] = s0_ref[...]

    seg = seg_ref[...]
    diag = dm_ref[...]

    steps = min(tt, SUBLANES)

    def block(tb, carry):
        rows = pl.ds(pl.multiple_of(tb * steps, steps), steps)
        for p in range(n_pairs):
            sl = slice(p * LANES, (p + 1) * LANES)
            r8, w8, k8, v8, a8, b8 = [ref[rows, sl] for ref in (r_ref, w_ref, k_ref, v_ref, a_ref, b_ref)]
            s = s_scr[:, sl]
            ys = []
            for i in range(steps):
                row = lambda x: x[i:i + 1, :]
                sa = _split_dot(s * row(a8), seg)
                vcol = _split_dot(diag * row(v8), seg)
                s = s * row(w8) + sa * row(b8) + vcol * row(k8)
                yb = _split_dot(s * row(r8), seg)
                ys.append(jnp.sum(yb * diag, axis=0, keepdims=True))
            s_scr[:, sl] = s
            y_ref[rows, sl] = jnp.concatenate(ys, axis=0)
        return carry

    lax.fori_loop(0, tt // steps, block, 0)

    @pl.when(tc == pl.num_programs(1) - 1)
    def _():
        sT_ref[...] = s_scr[...]


def _scan(rwkv, s0, grp, seg, diag):
    b, t = grp["b"], grp["t"]
    da = rwkv[0].shape[1]
    hd = s0.shape[1]
    tt = min(t, 256)
    seq_spec = pl.BlockSpec((None, tt, da), lambda i, j: (i, j, 0))
    st_spec = pl.BlockSpec((None, hd, da), lambda i, j: (i, 0, 0))
    y, s_fin = pl.pallas_call(
        functools.partial(_scan_kernel, tt, da // LANES), grid=(b, t // tt),
        in_specs=[seq_spec] * 6 + [st_spec, pl.BlockSpec((LANES, LANES), lambda i, j: (0, 0)),
                                   pl.BlockSpec((hd, LANES), lambda i, j: (0, 0))],
        out_specs=[seq_spec, st_spec],
        out_shape=[jax.ShapeDtypeStruct((b, t, da), F32), jax.ShapeDtypeStruct((b, hd, da), F32)],
        scratch_shapes=[pltpu.VMEM((hd, da), F32)],
        compiler_params=_cparams("arbitrary", "arbitrary"), name="wkv_scan",
    )(*[x.reshape(b, t, da) for x in rwkv], s0, seg, diag)
    return y.reshape(b * t, da), s_fin


def _sb_weights(z, vis, run, tri, *, keys_on_lanes):
    sp = _softplus(z)
    lk = -sp if vis is None else jnp.where(vis, -sp, 0.0)
    later = _split_dot(lk, tri, left=not keys_on_lanes) + run
    wts = jnp.exp(z - sp + later)
    if vis is not None:
        wts = jnp.where(vis, wts, 0.0)
    return wts, lk


def _sb_prompt_kernel(tq, hd, scale, bias_ref, q_ref, k_ref, v_ref, tri_ref, o_ref):
    p, i = pl.program_id(1), pl.program_id(2)
    lane = lax.broadcasted_iota(jnp.int32, (1, LANES), 1)
    q = q_ref[...]
    tri = tri_ref[...]
    qpos = i * tq + lax.broadcasted_iota(jnp.int32, (tq, 1), 0)
    kidx = lax.broadcasted_iota(jnp.int32, (1, tq), 1)
    acc = jnp.zeros((tq, LANES), F32)
    for hh in range(LANES // hd):
        hmask = (lane >= hh * hd) & (lane < (hh + 1) * hd)
        qh = jnp.where(hmask, q, 0.0).astype(BF16)
        bias = bias_ref[p * (LANES // hd) + hh]

        def body(s, carry):
            acc, run = carry
            j = i - s
            rows = pl.ds(pl.multiple_of(j * tq, tq), tq)
            z = _dot_nt(qh, k_ref[rows, :].astype(BF16)) * scale + bias
            vis = (j * tq + kidx) < qpos
            wts, lk = _sb_weights(z, vis, run, tri, keys_on_lanes=True)
            vh = jnp.where(hmask, v_ref[rows, :], 0.0).astype(BF16)
            acc = acc + jnp.dot(wts.astype(BF16), vh, preferred_element_type=F32)
            return acc, run + jnp.sum(lk, axis=1, keepdims=True)

        acc, _ = lax.fori_loop(0, i + 1, body, (acc, jnp.zeros((tq, 1), F32)))
    o_ref[...] = acc


def _sb_prompt(proj_main, grp, sb_bias, db, hd, q_off):
    b, t = grp["b"], grp["t"]
    n = proj_main.shape[0]
    tq = min(t, 256)
    nq = t // tq
    cb = q_off // LANES
    tri = (jnp.arange(tq)[:, None] > jnp.arange(tq)[None, :]).astype(BF16)
    return pl.pallas_call(
        functools.partial(_sb_prompt_kernel, tq, hd, hd ** -0.5),
        grid=(b, db // LANES, nq),
        in_specs=[pl.BlockSpec(memory_space=pltpu.SMEM),
                  pl.BlockSpec((tq, LANES), lambda bb, p, i: (bb * nq + i, cb + p)),
                  pl.BlockSpec((t, LANES), lambda bb, p, i: (bb, cb + db // LANES + p)),
                  pl.BlockSpec((t, LANES), lambda bb, p, i: (bb, cb + 2 * (db // LANES) + p)),
                  pl.BlockSpec((tq, tq), lambda bb, p, i: (0, 0))],
        out_specs=pl.BlockSpec((tq, LANES), lambda bb, p, i: (bb * nq + i, p)),
        out_shape=jax.ShapeDtypeStruct((n, db), F32),
        compiler_params=_cparams("arbitrary", "arbitrary", "arbitrary"), name="sb_prompt",
    )(sb_bias, proj_main, proj_main, proj_main, tri)


def _sb_sample_kernel(t_new, tp, n_heads, hd, page, scale, pt_ref, q_ref, kn_ref, vn_ref, kc_ref, vc_ref,
                      bias_ref, hm_ref, tri_ref, o_ref, qf_scr, kn_scr, vn_scr, acc_scr, run_scr):
    j = pl.program_id(1)
    nc = n_heads * tp
    hmask = hm_ref[...]
    tri = tri_ref[...]

    def block(k3, v3, vis):
        k2 = k3.reshape(page * n_heads, hd).astype(BF16)
        zall = _dot_nt(k2, qf_scr[...])
        z = jnp.sum(zall.reshape(page, n_heads, nc) * hmask[None], axis=1) * scale + bias_ref[...]
        wts, lk = _sb_weights(z, vis, run_scr[...], tri, keys_on_lanes=False)
        wexp = (wts[:, None, :] * hmask[None]).reshape(page * n_heads, nc).astype(BF16)
        v2 = v3.reshape(page * n_heads, hd).astype(BF16)
        acc_scr[...] += _dot_tn(wexp, v2)
        run_scr[...] += jnp.sum(lk, axis=0, keepdims=True)

    @pl.when(j == 0)
    def _():
        qf_scr[...] = jnp.zeros_like(qf_scr)
        q = q_ref[...]
        for h in range(n_heads):
            qf_scr[h * tp:h * tp + t_new, :] = q[:, h * hd:(h + 1) * hd].astype(BF16)
        kn_scr[...] = jnp.zeros_like(kn_scr)
        vn_scr[...] = jnp.zeros_like(vn_scr)
        kn_scr[0:tp] = kn_ref[...]
        vn_scr[0:tp] = vn_ref[...]
        acc_scr[...] = jnp.zeros_like(acc_scr)
        run_scr[...] = jnp.zeros_like(run_scr)
        s_idx = lax.broadcasted_iota(jnp.int32, (page, nc), 0)
        t_idx = lax.rem(lax.broadcasted_iota(jnp.int32, (page, nc), 1), tp)
        block(kn_scr[...], vn_scr[...], s_idx < t_idx)

    block(kc_ref[...], vc_ref[...], None)

    @pl.when(j == pl.num_programs(1) - 1)
    def _():
        acc = acc_scr[...]
        for h in range(n_heads):
            o_ref[:, h * hd:(h + 1) * hd] = acc[h * tp:h * tp + t_new, :]


def _sb_sample(proj_main, grp, cache_k, cache_v, layer, page_table, sb_bias, db, hd, q_off):
    b, t = grp["b"], grp["t"]
    n_heads = db // hd
    page = cache_k.shape[2]
    n_pages = page_table.shape[1]
    tp = 8
    assert t <= tp
    nc = n_heads * tp
    p3 = proj_main.reshape(b, t, proj_main.shape[1])
    pad_new = lambda x: jnp.pad(x.reshape(b, t, n_heads, hd), ((0, 0), (0, tp - t), (0, 0), (0, 0)))
    k_new = pad_new(lax.slice_in_dim(proj_main, q_off + db, q_off + 2 * db, axis=1))
    v_new = pad_new(lax.slice_in_dim(proj_main, q_off + 2 * db, q_off + 3 * db, axis=1))
    bias_c = jnp.repeat(sb_bias, tp).reshape(1, nc)
    hmask = (jnp.arange(n_heads)[:, None] == (jnp.arange(nc) // tp)[None, :]).astype(F32)
    tri = (jnp.arange(page)[None, :] > jnp.arange(page)[:, None]).astype(BF16)
    new_spec = pl.BlockSpec((None, tp, n_heads, hd), lambda bb, j, pt: (bb, 0, 0, 0))
    page_spec = pl.BlockSpec((None, None, page, n_heads, hd),
                             lambda bb, j, pt: (layer, pt[bb * n_pages + n_pages - 1 - j], 0, 0, 0))
    const = lambda shape: pl.BlockSpec(shape, lambda bb, j, pt: (0,) * len(shape))
    out = pl.pallas_call(
        functools.partial(_sb_sample_kernel, t, tp, n_heads, hd, page, hd ** -0.5),
        grid_spec=pltpu.PrefetchScalarGridSpec(
            num_scalar_prefetch=1, grid=(b, n_pages),
            in_specs=[pl.BlockSpec((None, t, db), lambda bb, j, pt: (bb, 0, q_off // db)),
                      new_spec, new_spec, page_spec, page_spec,
                      const((1, nc)), const((n_heads, nc)), const((page, page))],
            out_specs=pl.BlockSpec((None, t, db), lambda bb, j, pt: (bb, 0, 0)),
            scratch_shapes=[pltpu.VMEM((nc, hd), BF16), pltpu.VMEM((page, n_heads, hd), F32),
                            pltpu.VMEM((page, n_heads, hd), F32), pltpu.VMEM((nc, hd), F32),
                            pltpu.VMEM((1, nc), F32)]),
        out_shape=jax.ShapeDtypeStruct((b, t, db), F32),
        compiler_params=_cparams("arbitrary", "arbitrary"), name="sb_sample",
    )(page_table.reshape(-1), p3, k_new, v_new, cache_k, cache_v, bias_c, hmask, tri)
    return out.reshape(b * t, db)


def _merge_kernel(y_ref, r_ref, k_ref, v_ref, og_ref, yb_ref, ga_ref, gb_ref, wa_ref, wb_ref,
                  rk_ref, lg_ref, lb_ref, seg_ref, o_ref, ya_scr):
    @pl.when(pl.program_id(1) == 0)
    def _():
        seg = seg_ref[...]
        y = y_ref[...]
        hd_inv = 1.0 / jnp.sum(seg[0:1, :].astype(F32))
        mu = _head_sum(y, seg) * hd_inv
        yc = y - mu
        var = _head_sum(yc * yc, seg) * hd_inv
        yn = yc * lax.rsqrt(var + GN_EPS) * lg_ref[...] + lb_ref[...]
        bonus = _head_sum(r_ref[...] * k_ref[...] * rk_ref[...], seg)
        ya_scr[...] = ((yn + bonus * v_ref[...]) * og_ref[...]).astype(BF16)
    ma = jnp.dot(ya_scr[...], wa_ref[...], preferred_element_type=F32)
    mb = jnp.dot(yb_ref[...].astype(BF16), wb_ref[...], preferred_element_type=F32)
    o_ref[...] = (_sigmoid(ga_ref[...]) * ma + _sigmoid(gb_ref[...]) * mb).astype(BF16)


def _merge(y, rwkv_r, rwkv_k, rwkv_v, og, y_b, proj_main, gate_off, wa_bf, wb_bf, prm, seg, tm):
    n, da = y.shape
    db = y_b.shape[1]
    d = wa_bf.shape[1]
    tn = _tile(d, 512)
    ga_blk = gate_off // tn
    row = lambda w: pl.BlockSpec((tm, w), lambda i, j: (i, 0))
    vec = lambda w: pl.BlockSpec((1, w), lambda i, j: (0, 0))
    return pl.pallas_call(
        _merge_kernel, grid=(n // tm, d // tn),
        in_specs=[row(da)] * 5 + [row(db),
                  pl.BlockSpec((tm, tn), lambda i, j: (i, ga_blk + j)),
                  pl.BlockSpec((tm, tn), lambda i, j: (i, ga_blk + d // tn + j)),
                  pl.BlockSpec((da, tn), lambda i, j: (0, j)), pl.BlockSpec((db, tn), lambda i, j: (0, j)),
                  vec(da), vec(da), vec(da), pl.BlockSpec((LANES, LANES), lambda i, j: (0, 0))],
        out_specs=pl.BlockSpec((tm, tn), lambda i, j: (i, j)),
        out_shape=jax.ShapeDtypeStruct((n, d), BF16),
        scratch_shapes=[pltpu.VMEM((tm, da), BF16)],
        compiler_params=_cparams("arbitrary", "arbitrary"), name="merge",
    )(y, rwkv_r, rwkv_k, rwkv_v, og, y_b, proj_main, proj_main, wa_bf, wb_bf,
      prm["r_k"], prm["lnx_g"], prm["lnx_b"], seg)


def _wo_ln_kernel(alpha, m_ref, w_ref, x_ref, gm_ref, shf_ref, scf_ref, lg_ref, lb_ref, x1_o, h2_o):
    y = jnp.dot(m_ref[...], w_ref[...], preferred_element_type=F32)
    x1 = _std(alpha * x_ref[...] + gm_ref[...] * y, LN_EPS) * lg_ref[...] + lb_ref[...]
    x1_o[...] = x1
    h2_o[...] = (_std(x1, LN_EPS) * (1.0 + scf_ref[...]) + shf_ref[...]).astype(BF16)


def _wo_ln(merged, x, grp, mod, wo_bf, ln_g, ln_b, alpha, tm):
    n, d = x.shape
    row = pl.BlockSpec((tm, d), lambda i: (i, 0))
    vec = pl.BlockSpec((1, d), lambda i: (0, 0))
    return pl.pallas_call(
        functools.partial(_wo_ln_kernel, alpha), grid=(n // tm,),
        in_specs=[row, pl.BlockSpec((d, d), lambda i: (0, 0)), row,
                  _mod_spec(grp, 2, tm), _mod_spec(grp, 3, tm), _mod_spec(grp, 4, tm), vec, vec],
        out_specs=[row, row],
        out_shape=[jax.ShapeDtypeStruct((n, d), F32), jax.ShapeDtypeStruct((n, d), BF16)],
        compiler_params=_cparams("arbitrary"), name="wo_ln",
    )(merged, wo_bf, x, mod, mod, mod, ln_g, ln_b)


def _peer_q_kernel(ng, hk, h_ref, wq_ref, keys_ref, o_ref):
    qp = jnp.dot(h_ref[...], wq_ref[...], preferred_element_type=F32)
    for g in range(ng):
        o_ref[g] = _dot_nt(keys_ref[g], qp[:, g * hk:(g + 1) * hk].astype(BF16))


def _peer_q(h2, wq_bf, keys_bf, tm):
    n, d = h2.shape
    n_grp, nk, hk = keys_bf.shape
    ng = 4 if n_grp % 4 == 0 else 2
    return pl.pallas_call(
        functools.partial(_peer_q_kernel, ng, hk), grid=(n // tm, n_grp // ng),
        in_specs=[pl.BlockSpec((tm, d), lambda i, j: (i, 0)),
                  pl.BlockSpec((d, ng * hk), lambda i, j: (0, j)),
                  pl.BlockSpec((ng, nk, hk), lambda i, j: (j, 0, 0))],
        out_specs=pl.BlockSpec((ng, nk, tm), lambda i, j: (j, 0, i)),
        out_shape=jax.ShapeDtypeStruct((n_grp, nk, n), F32),
        compiler_params=_cparams("arbitrary", "arbitrary"), name="peer_q",
    )(h2, wq_bf, keys_bf)


def _top_rows(s, k, payloads=()):
    n_rows = s.shape[0]
    rid = lax.broadcasted_iota(jnp.int32, s.shape, 0).astype(F32)
    vals, idxs, picked = [], [], [[] for _ in payloads]
    for _ in range(k):
        m = jnp.max(s, axis=0, keepdims=True)
        idx = jnp.min(jnp.where(s == m, rid, float(n_rows)), axis=0, keepdims=True)
        hit = rid == idx
        vals.append(m)
        idxs.append(idx)
        for out, pay in zip(picked, payloads):
            out.append(jnp.max(jnp.where(hit, pay, -1.0), axis=0, keepdims=True))
        s = jnp.where(hit, -jnp.inf, s)
    return vals, idxs, picked


def _peer_topk_kernel(s_ref, g_o, i1_o, i2_o):
    tl = s_ref.shape[2]
    v0, i0, _ = _top_rows(s_ref[0], TOPK)
    v1, i1, _ = _top_rows(s_ref[1], TOPK)
    v1c = jnp.concatenate(v1, axis=0)
    i1c = jnp.concatenate(i1, axis=0)
    cand = jnp.concatenate([v0[i] + v1c for i in range(TOPK)], axis=0)
    c1 = jnp.concatenate([jnp.broadcast_to(i0[i], (TOPK, tl)) for i in range(TOPK)], axis=0)
    c2 = jnp.concatenate([i1c] * TOPK, axis=0)
    best, _, (e1, e2) = _top_rows(cand, TOPK, (c1, c2))
    best = jnp.concatenate(best, axis=0)
    e = jnp.exp(best - best[0:1])
    g_o[...] = e / jnp.sum(e, axis=0, keepdims=True)
    i1_o[...] = jnp.concatenate(e1, axis=0)
    i2_o[...] = jnp.concatenate(e2, axis=0)


def _peer_topk(scores_t):
    n_grp, nk, n = scores_t.shape
    n_heads = n_grp // 2
    tl = _tile(n, LANES)
    out = jax.ShapeDtypeStruct((n_heads * TOPK, n), F32)
    ospec = pl.BlockSpec((TOPK, tl), lambda i, h: (h, i))
    return pl.pallas_call(
        _peer_topk_kernel, grid=(n // tl, n_heads),
        in_specs=[pl.BlockSpec((2, nk, tl), lambda i, h: (h, 0, i))],
        out_specs=[ospec] * 3, out_shape=[out] * 3,
        compiler_params=_cparams("arbitrary", "arbitrary"), name="peer_topk",
    )(scores_t)


def _peer_gate_kernel(tg, nk, g_ref, i1_ref, i2_ref, o_ref, g_s, i1_s, i2_s):
    g_s[...] = g_ref[...].T
    i1_s[...] = i1_ref[...].T
    i2_s[...] = i2_ref[...].T
    m = g_s.shape[1]
    sub = lax.broadcasted_iota(jnp.int32, (nk, m), 0).astype(F32)

    steps = min(tg, SUBLANES)

    def body(nb, carry):
        base = pl.multiple_of(nb * steps, steps)
        rows = pl.ds(base, steps)
        g8, a8, b8 = g_s[rows, :], i1_s[rows, :], i2_s[rows, :]
        for i in range(steps):
            a_t = jnp.where(a8[i:i + 1, :] == sub, g8[i:i + 1, :], 0.0).astype(BF16)
            b_t = jnp.where(b8[i:i + 1, :] == sub, 1.0, 0.0).astype(BF16)
            o_ref[base + i] = _dot_nt(a_t, b_t).astype(BF16)
        return carry

    lax.fori_loop(0, tg // steps, body, 0)


def _peer_gate(gate_t, i1_t, i2_t, nk):
    m, n = gate_t.shape
    tg = _tile(n, LANES)
    ispec = pl.BlockSpec((m, tg), lambda i: (0, i))
    return pl.pallas_call(
        functools.partial(_peer_gate_kernel, tg, nk), grid=(n // tg,),
        in_specs=[ispec] * 3,
        out_specs=pl.BlockSpec((tg, nk, nk), lambda i: (i, 0, 0)),
        out_shape=jax.ShapeDtypeStruct((n, nk, nk), BF16),
        scratch_shapes=[pltpu.VMEM((tg, m), F32)] * 3,
        compiler_params=_cparams("arbitrary"), name="peer_gate",
    )(gate_t, i1_t, i2_t)


def _gelu_tanh(x):
    return 0.5 * x * (1.0 + jnp.tanh(math.sqrt(2.0 / math.pi) * (x + 0.044715 * (x * x * x))))


def _peer_dense_kernel(alpha, h_ref, g_ref, ut_ref, v_ref, x1_ref, gf_ref, lg_ref, lb_ref, o_ref):
    c = pl.program_id(1)

    @pl.when(c == 0)
    def _():
        o_ref[...] = jnp.zeros_like(o_ref)

    act = _gelu_tanh(jnp.dot(h_ref[...], ut_ref[...], preferred_element_type=F32))
    p = (act * g_ref[...].astype(F32)).astype(BF16)
    o_ref[...] += jnp.dot(p, v_ref[...], preferred_element_type=F32)

    @pl.when(c == pl.num_programs(1) - 1)
    def _():
        z = alpha * x1_ref[...] + gf_ref[...] * o_ref[...]
        o_ref[...] = _std(z, LN_EPS) * lg_ref[...] + lb_ref[...]


def _peer_dense(h2, gmat, ut_bf, v_bf, x1, grp, mod, ln_g, ln_b, alpha, tm):
    n, d = x1.shape
    e = v_bf.shape[0]
    te = _tile(e, 512)
    row = pl.BlockSpec((tm, d), lambda i, c: (i, 0))
    vec = pl.BlockSpec((1, d), lambda i, c: (0, 0))
    return pl.pallas_call(
        functools.partial(_peer_dense_kernel, alpha), grid=(n // tm, e // te),
        in_specs=[row, pl.BlockSpec((tm, te), lambda i, c: (i, c)),
                  pl.BlockSpec((d, te), lambda i, c: (0, c)), pl.BlockSpec((te, d), lambda i, c: (c, 0)),
                  row, _mod_spec(grp, 5, tm), vec, vec],
        out_specs=row, out_shape=jax.ShapeDtypeStruct((n, d), F32),
        compiler_params=_cparams("arbitrary", "arbitrary"), name="peer_dense",
    )(h2, gmat, ut_bf, v_bf, x1, mod, ln_g, ln_b)


def _run_trunk(x, grp, mods, shift0, wkv0, cache_k, cache_v, page_table, w, dims):
    d, da, db, hd, lp, depth = dims["d"], dims["da"], dims["db"], dims["hd"], dims["lp"], dims["depth"]
    b, t = grp["b"], grp["t"]
    n = b * t
    tm = grp["tm"]
    alpha = (2 * depth) ** 0.25
    n_raw = 3 * da + dims["n_lora"]
    seg, diag = w["seg"], w["diag"]
    k_rows, v_rows, wkv_rows, shift_rows = [], [], [], []
    for l in range(depth):
        mod = mods[l]
        lw = w["layers"][l]
        proj_main = _proj(x, grp, mod, lw["w_main"], _tile(da, 512))
        proj_lora = _proj(x, grp, mod, lw["w_lora"], lp)
        if grp["per_row"]:
            prev_main = jnp.repeat(shift0[l][:, :3 * da], t, axis=0)
            prev_lora = jnp.repeat(jnp.pad(shift0[l][:, 3 * da:], ((0, 0), (0, lp - dims["n_lora"]))), t, axis=0)
        else:
            prev_main = shift0[l][:, None, :3 * da]
            prev_lora = jnp.pad(shift0[l][:, None, 3 * da:], ((0, 0), (0, 0), (0, lp - dims["n_lora"])))
        r_, w_, k_, v_, a_, b_, og = _rwkv_prep(proj_main, proj_lora, grp, prev_main, prev_lora, lw, seg)
        s0 = wkv0[l].transpose(0, 2, 1, 3).reshape(b, hd, da)
        y, s_fin = _scan((r_, w_, k_, v_, a_, b_), s0, grp, seg, diag)
        if cache_k is None:
            y_b = _sb_prompt(proj_main, grp, lw["sb_bias"], db, hd, 3 * da)
        else:
            y_b = _sb_sample(proj_main, grp, cache_k, cache_v, l, page_table, lw["sb_bias"], db, hd, 3 * da)
        merged = _merge(y, r_, k_, v_, og, y_b, proj_main, 3 * da + 3 * db, lw["w_br_a"], lw["w_br_b"], lw, seg,
                        grp["tm_prep"])
        x1, h2 = _wo_ln(merged, x, grp, mod, lw["wo"], lw["ln1_g"], lw["ln1_b"], alpha, grp["tm_prep"])
        scores_t = _peer_q(h2, lw["wq"], lw["keys"], tm)
        gate_t, i1_t, i2_t = _peer_topk(scores_t)
        nk = lw["keys"].shape[1]
        gmat = _peer_gate(gate_t, i1_t, i2_t, nk).reshape(n, nk * nk)
        x = _peer_dense(h2, gmat, lw["ut"], lw["v"], x1, grp, mod, lw["ln2_g"], lw["ln2_b"], alpha, tm)
        p3 = proj_main.reshape(b, t, -1)
        k_rows.append(p3[:, :, 3 * da + db:3 * da + 2 * db].reshape(b, t, db // hd, hd))
        v_rows.append(p3[:, :, 3 * da + 2 * db:3 * da + 3 * db].reshape(b, t, db // hd, hd))
        wkv_rows.append(s_fin.reshape(b, hd, da // hd, hd).transpose(0, 2, 1, 3))
        shift_rows.append(jnp.concatenate(
            [p3[:, -1, :3 * da], proj_lora.reshape(b, t, lp)[:, -1, :dims["n_lora"]]], axis=-1))
        assert shift_rows[-1].shape[-1] == n_raw
    return (x.reshape(b, t, d), jnp.stack(k_rows), jnp.stack(v_rows), jnp.stack(wkv_rows), jnp.stack(shift_rows))


def kernel(x_prompt, x_sample, cache_k, cache_v, state_wkv, state_shift, page_table, c_prompt, c_sample,
           w_ada, b_ada, w_in, mu_shift, decay_bias, decay_up, aaa_bias, aaa_up, og_up, k_k, k_a, r_k,
           lnx_g, lnx_b, sb_bias, w_branch_a, w_branch_b, w_o, ln1_g, ln1_b, peer_wq, peer_keys, peer_u, peer_v,
           ln2_g, ln2_b):
    depth, d, _ = w_ada.shape
    hd = cache_k.shape[-1]
    da = decay_up.shape[2]
    db = w_branch_b.shape[1]
    wl, al, gl = decay_up.shape[1], aaa_up.shape[1], og_up.shape[1]
    n_lora = wl + al + gl
    lp = -(-n_lora // LANES) * LANES
    assert da % LANES == 0 and db % LANES == 0 and LANES % hd == 0
    dims = dict(d=d, da=da, db=db, hd=hd, lp=lp, depth=depth, n_lora=n_lora)

    seg = (jnp.arange(LANES)[:, None] // hd == jnp.arange(LANES)[None, :] // hd).astype(BF16)
    diag = (jnp.arange(hd)[:, None] == jnp.arange(LANES)[None, :] % hd).astype(F32)
    layers = []
    for l in range(depth):
        rw = 3 * da + n_lora
        pad_rows = lambda m, off: jnp.pad(m, ((off, lp - off - m.shape[0]), (0, 0))).astype(BF16)
        n_grp = peer_keys.shape[1] * peer_keys.shape[2]
        layers.append(dict(
            da=da, lp=lp,
            w_main=jnp.concatenate([w_in[l][:, :3 * da], w_in[l][:, rw:]], axis=1).astype(BF16),
            w_lora=jnp.pad(w_in[l][:, 3 * da:rw], ((0, 0), (0, lp - n_lora))).astype(BF16),
            mu_main=mu_shift[l][None, :3 * da],
            mu_lora=jnp.pad(mu_shift[l][None, 3 * da:], ((0, 0), (0, lp - n_lora))),
            decay_bias=decay_bias[l][None], aaa_bias=aaa_bias[l][None],
            wd=pad_rows(decay_up[l], 0), wa=pad_rows(aaa_up[l], wl), wg=pad_rows(og_up[l], wl + al),
            w_br_a=w_branch_a[l].astype(BF16), w_br_b=w_branch_b[l].astype(BF16),
            k_k=k_k[l][None], k_a=k_a[l][None], r_k=r_k[l].reshape(1, da),
            lnx_g=lnx_g[l][None], lnx_b=lnx_b[l][None], sb_bias=sb_bias[l],
            wo=w_o[l].astype(BF16), ln1_g=ln1_g[l][None], ln1_b=ln1_b[l][None],
            wq=peer_wq[l].astype(BF16), keys=peer_keys[l].reshape(n_grp, *peer_keys.shape[3:]).astype(BF16),
            ut=peer_u[l].T.astype(BF16), v=peer_v[l].astype(BF16),
            ln2_g=ln2_g[l][None], ln2_b=ln2_b[l][None]))
    w = dict(seg=seg, diag=diag, layers=layers)

    bp, tp_, _ = x_prompt.shape
    bs, ts, _ = x_sample.shape
    mod_all = _ada(jnp.concatenate([c_prompt, c_sample], axis=0), w_ada.astype(BF16), b_ada)

    def group(b, t):
        n = b * t
        tm = _tile(n, 512) if t < 512 else _tile(t, 512)
        tm_prep = _tile(n, 256) if t < 256 else _tile(t, 256)
        return dict(b=b, t=t, d=d, tm=tm, tm_prep=tm_prep, per_row=t < tm)

    gp, gs = group(bp, tp_), group(bs, ts)
    assert gp["per_row"] == (gp["t"] < gp["tm_prep"]) and gs["per_row"] == (gs["t"] < gs["tm_prep"])

    def mods_for(grp, lo, hi):
        out = []
        for l in range(depth):
            m = mod_all[l, lo:hi]
            out.append(jnp.repeat(m, grp["t"], axis=0) if grp["per_row"] else m.reshape(hi - lo, 6, 1, d))
        return out

    shift0_p = jnp.zeros((depth, bp, 3 * da + n_lora), F32)
    wkv0_p = jnp.zeros((depth, bp, da // hd, hd, hd), F32)
    y_p, k_p, v_p, wkv_p, shift_p = _run_trunk(
        x_prompt.reshape(bp * tp_, d), gp, mods_for(gp, 0, bp), shift0_p, wkv0_p, None, None, None, w, dims)
    y_s, k_s, v_s, wkv_s, shift_s = _run_trunk(
        x_sample.reshape(bs * ts, d), gs, mods_for(gs, bp, bp + bs), state_shift, state_wkv,
        cache_k, cache_v, page_table, w, dims)
    return (y_p, y_s, k_p, v_p, wkv_p, shift_p, k_s, v_s, wkv_s, shift_s)
```

```python
import functools
import math

import jax
import jax.numpy as jnp
from jax import lax
from jax.experimental import pallas as pl
from jax.experimental.pallas import tpu as pltpu

F32 = jnp.float32
BF16 = jnp.bfloat16
LN_EPS = 1e-5
GN_EPS = 64e-5
KK_EPS = 1e-12
TOPK = 16
LANES = 128
SUBLANES = 8
BF16_ROWS = 16
RID_NONE = 1e9
VMEM_LIMIT = 56 * 1024 * 1024


def _cparams(*sem):
    return pltpu.CompilerParams(dimension_semantics=sem, vmem_limit_bytes=VMEM_LIMIT)


def _tile(n, pref):
    if n <= pref:
        return n
    t = (pref // LANES) * LANES
    while t > LANES and n % t:
        t -= LANES
    assert n % t == 0, (n, pref)
    return t


def _std(x, eps):
    mu = jnp.mean(x, axis=-1, keepdims=True)
    xc = x - mu
    var = jnp.mean(xc * xc, axis=-1, keepdims=True)
    return xc * lax.rsqrt(var + eps)


def _sigmoid(x):
    return 1.0 / (1.0 + jnp.exp(-x))


def _softplus(x):
    return jnp.maximum(x, 0.0) + jnp.log(1.0 + jnp.exp(-jnp.abs(x)))


def _split_dot(x, m, *, left=False):
    hi = x.astype(BF16)
    lo = (x - hi.astype(F32)).astype(BF16)
    if left:
        return (jnp.dot(m, hi, preferred_element_type=F32) + jnp.dot(m, lo, preferred_element_type=F32))
    return (jnp.dot(hi, m, preferred_element_type=F32) + jnp.dot(lo, m, preferred_element_type=F32))


def _head_sum(x, seg):
    outs = [_split_dot(x[:, c * LANES:(c + 1) * LANES], seg) for c in range(x.shape[1] // LANES)]
    return outs[0] if len(outs) == 1 else jnp.concatenate(outs, axis=1)


def _dot_nt(a, b):
    return lax.dot_general(a, b, (((1,), (1,)), ((), ())), preferred_element_type=F32)


def _dot_tn(a, b):
    return lax.dot_general(a, b, (((0,), (0,)), ((), ())), preferred_element_type=F32)


def _ada_kernel(c_ref, w_ref, b_ref, o_ref):
    c = c_ref[...]
    s = (c * _sigmoid(c)).astype(BF16)
    o_ref[...] = jnp.dot(s, w_ref[...], preferred_element_type=F32) + b_ref[...]


def _ada(c_all, w_bf, b_ada):
    depth, d, d6 = w_bf.shape
    nb = c_all.shape[0]
    tn = _tile(d6, 1024)
    return pl.pallas_call(
        _ada_kernel, grid=(depth, d6 // tn),
        in_specs=[pl.BlockSpec((nb, d), lambda l, j: (0, 0)),
                  pl.BlockSpec((None, d, tn), lambda l, j: (l, 0, j)),
                  pl.BlockSpec((None, 1, tn), lambda l, j: (l, 0, j))],
        out_specs=pl.BlockSpec((None, nb, tn), lambda l, j: (l, 0, j)),
        out_shape=jax.ShapeDtypeStruct((depth, nb, d6), F32),
        compiler_params=_cparams("arbitrary", "arbitrary"), name="ada",
    )(c_all, w_bf, b_ada.reshape(depth, 1, d6))


def _mod_spec(grp, which, tm):
    d = grp["d"]
    if grp["per_row"]:
        return pl.BlockSpec((tm, d), lambda i, *_: (i, which))
    t = grp["t"]
    return pl.BlockSpec((None, None, 1, d), lambda i, *_: ((i * tm) // t, which, 0, 0))


def _proj_kernel(x_ref, sh_ref, sc_ref, w_ref, o_ref, h_scr):
    @pl.when(pl.program_id(1) == 0)
    def _():
        h = _std(x_ref[...], LN_EPS) * (1.0 + sc_ref[...]) + sh_ref[...]
        h_scr[...] = h.astype(BF16)
    o_ref[...] = jnp.dot(h_scr[...], w_ref[...], preferred_element_type=F32)


def _proj(x, grp, mod, w_bf, tn):
    n, d = x.shape
    p = w_bf.shape[1]
    tm = grp["tm"]
    return pl.pallas_call(
        _proj_kernel, grid=(n // tm, p // tn),
        in_specs=[pl.BlockSpec((tm, d), lambda i, j: (i, 0)),
                  _mod_spec(grp, 0, tm), _mod_spec(grp, 1, tm),
                  pl.BlockSpec((d, tn), lambda i, j: (0, j))],
        out_specs=pl.BlockSpec((tm, tn), lambda i, j: (i, j)),
        out_shape=jax.ShapeDtypeStruct((n, p), F32),
        scratch_shapes=[pltpu.VMEM((tm, d), BF16)],
        compiler_params=_cparams("arbitrary", "arbitrary"), name="proj",
    )(x, mod, mod, w_bf)


def _rwkv_prep_kernel(t_seq, tm, da, per_row, pm_ref, pl_ref, prevm_ref, prevl_ref, mum_ref, mul_ref,
                      dbias_ref, abias_ref, wd_ref, wa_ref, wg_ref, kk_ref, ka_ref, seg_ref,
                      r_o, w_o, k_o, v_o, a_o, b_o, og_o, carm, carl):
    i = pl.program_id(0)
    row = lax.broadcasted_iota(jnp.int32, (tm, 1), 0)

    def shifted(p, prev_ref, car):
        rolled = pltpu.roll(p, 1, 0)
        if per_row:
            return jnp.where(lax.rem(row, t_seq) == 0, prev_ref[...], rolled)
        first = jnp.where(lax.rem(i, t_seq // tm) == 0, prev_ref[...], car[...])
        car[...] = p[tm - 1:tm, :]
        return jnp.where(row == 0, first, rolled)

    p = pm_ref[...]
    pm = p + (shifted(p, prevm_ref, carm) - p) * mum_ref[...]
    q = pl_ref[...]
    lo = q + (shifted(q, prevl_ref, carl) - q) * mul_ref[...]
    r = pm[:, :da]
    k = pm[:, da:2 * da]
    v = pm[:, 2 * da:3 * da]
    w_pre = dbias_ref[...] + jnp.dot(jnp.tanh(lo).astype(BF16), wd_ref[...], preferred_element_type=F32)
    decay = jnp.exp(-math.exp(-0.5) * _sigmoid(w_pre))
    a = _sigmoid(abias_ref[...] + jnp.dot(lo.astype(BF16), wa_ref[...], preferred_element_type=F32))
    og = jnp.dot(_sigmoid(lo).astype(BF16), wg_ref[...], preferred_element_type=F32)
    kk = k * kk_ref[...]
    kk = kk * lax.rsqrt(_head_sum(kk * kk, seg_ref[...]) + KK_EPS)
    r_o[...] = r
    w_o[...] = decay
    k_o[...] = k * (1.0 + (a - 1.0) * ka_ref[...])
    v_o[...] = v
    a_o[...] = -kk
    b_o[...] = kk * a
    og_o[...] = og


def _rwkv_prep(proj_main, proj_lora, grp, prev_main, prev_lora, prm, seg):
    n = proj_main.shape[0]
    da, lp = prm["da"], prm["lp"]
    tm = grp["tm_prep"]
    per_row = grp["t"] < tm
    row_spec = lambda w: pl.BlockSpec((tm, w), lambda i: (i, 0))
    vec_spec = lambda w: pl.BlockSpec((1, w), lambda i: (0, 0))
    if per_row:
        prev_specs = [row_spec(3 * da), row_spec(lp)]
    else:
        t = grp["t"]
        prev_specs = [pl.BlockSpec((None, 1, 3 * da), lambda i: ((i * tm) // t, 0, 0)),
                      pl.BlockSpec((None, 1, lp), lambda i: ((i * tm) // t, 0, 0))]
    out = jax.ShapeDtypeStruct((n, da), F32)
    return pl.pallas_call(
        functools.partial(_rwkv_prep_kernel, grp["t"], tm, da, per_row), grid=(n // tm,),
        in_specs=[row_spec(3 * da), row_spec(lp)] + prev_specs + [
            vec_spec(3 * da), vec_spec(lp), vec_spec(da), vec_spec(da),
            pl.BlockSpec((lp, da), lambda i: (0, 0)), pl.BlockSpec((lp, da), lambda i: (0, 0)),
            pl.BlockSpec((lp, da), lambda i: (0, 0)), vec_spec(da), vec_spec(da),
            pl.BlockSpec((LANES, LANES), lambda i: (0, 0))],
        out_specs=[row_spec(da)] * 7, out_shape=[out] * 7,
        scratch_shapes=[pltpu.VMEM((1, 3 * da), F32), pltpu.VMEM((1, lp), F32)],
        compiler_params=_cparams("arbitrary"), name="rwkv_prep",
    )(proj_main, proj_lora, prev_main, prev_lora, prm["mu_main"], prm["mu_lora"], prm["decay_bias"],
      prm["aaa_bias"], prm["wd"], prm["wa"], prm["wg"], prm["k_k"], prm["k_a"], seg)


def _scan_kernel(tt, n_pairs, r_ref, w_ref, k_ref, v_ref, a_ref, b_ref, s0_ref, seg_ref, dm_ref,
                 y_ref, sT_ref, s_scr):
    tc = pl.program_id(1)

    @pl.when(tc == 0)
    def _():
        s_scr[...] = s0_ref[...]

    seg = seg_ref[...]
    diag = dm_ref[...]

    steps = min(tt, SUBLANES)

    def block(tb, carry):
        rows = pl.ds(pl.multiple_of(tb * steps, steps), steps)
        r8, w8, k8, v8, a8, b8 = [ref[rows, :] for ref in (r_ref, w_ref, k_ref, v_ref, a_ref, b_ref)]
        ys = []
        for i in range(steps):
            yrow = []
            for p in range(n_pairs):
                sl = slice(p * LANES, (p + 1) * LANES)
                row = lambda x: x[i:i + 1, sl]
                s = s_scr[:, sl]
                sa = _split_dot(s * row(a8), seg)
                vcol = _split_dot(diag * row(v8), seg)
                s = s * row(w8) + sa * row(b8) + vcol * row(k8)
                s_scr[:, sl] = s
                yb = _split_dot(s * row(r8), seg)
                yrow.append(jnp.sum(yb * diag, axis=0, keepdims=True))
            ys.append(yrow[0] if n_pairs == 1 else jnp.concatenate(yrow, axis=1))
        y_ref[rows, :] = jnp.concatenate(ys, axis=0)
        return carry

    lax.fori_loop(0, tt // steps, block, 0)

    @pl.when(tc == pl.num_programs(1) - 1)
    def _():
        sT_ref[...] = s_scr[...]


def _wkv_chunk_kernel(n_sub, n_pairs, c, hd, r_ref, w_ref, k_ref, v_ref, a_ref, b_ref, h0_ref, y_ref, hT_ref, h_scr):
    tc = pl.program_id(1)
    nh = LANES // hd
    rows_st = nh * c

    @pl.when(tc == 0)
    def _():
        h_scr[...] = h0_ref[...]

    lane_head = lax.broadcasted_iota(jnp.int32, (1, LANES), 1) // hd
    ri = lax.broadcasted_iota(jnp.int32, (rows_st, rows_st), 0)
    ci = lax.broadcasted_iota(jnp.int32, (rows_st, rows_st), 1)
    same_head = (ri // c) == (ci // c)
    strict = same_head & (ci < ri)
    incl = same_head & (ci <= ri)
    ti = lax.broadcasted_iota(jnp.int32, (c, c), 0)
    tj = lax.broadcasted_iota(jnp.int32, (c, c), 1)
    tri_incl = jnp.where(tj <= ti, 1.0, 0.0).astype(BF16)
    ones_c = jnp.ones((c, LANES), BF16)

    def stack(x):
        return jnp.concatenate([jnp.where(lane_head == h, x, 0.0) for h in range(nh)], axis=0).astype(BF16)

    def dot(a, b):
        return jnp.dot(a, b, preferred_element_type=F32)

    def sub_chunk(sc, carry):
        rows = pl.ds(pl.multiple_of(sc * c, c), c)
        pairs = range(n_pairs)
        sls = [slice(p * LANES, (p + 1) * LANES) for p in pairs]
        ld = lambda ref: [ref[rows, sl] for sl in sls]
        r, w, k, v, a, b = ld(r_ref), ld(w_ref), ld(k_ref), ld(v_ref), ld(a_ref), ld(b_ref)
        lw = [jnp.log(x) for x in w]
        cum = [_split_dot(x, tri_incl, left=True) for x in lw]
        p_inv = [jnp.exp(-x) for x in cum]
        a_st = [stack(a[p] * jnp.exp(cum[p] - lw[p])) for p in pairs]
        r_st = [stack(r[p] * jnp.exp(cum[p])) for p in pairs]
        b_st = [stack(b[p] * p_inv[p]) for p in pairs]
        k_st = [stack(k[p] * p_inv[p]) for p in pairs]
        v_st = [stack(x) for x in v]
        h = [h_scr[p] for p in pairs]
        hb = [x.astype(BF16) for x in h]
        npow = [jnp.where(strict, _dot_nt(a_st[p], b_st[p]), 0.0).astype(BF16) for p in pairs]
        n_ak = [jnp.where(strict, _dot_nt(a_st[p], k_st[p]), 0.0).astype(BF16) for p in pairs]
        u = [dot(a_st[p], hb[p]) + dot(n_ak[p], v_st[p]) for p in pairs]
        span = 1
        while span < c:
            u = [u[p] + dot(npow[p], u[p].astype(BF16)) for p in pairs]
            span *= 2
            if span < c:
                npow = [dot(x, x).astype(BF16) for x in npow]
        ub = [x.astype(BF16) for x in u]
        m_rb = [jnp.where(incl, _dot_nt(r_st[p], b_st[p]), 0.0).astype(BF16) for p in pairs]
        m_rk = [jnp.where(incl, _dot_nt(r_st[p], k_st[p]), 0.0).astype(BF16) for p in pairs]
        for p in pairs:
            y_st = dot(r_st[p], hb[p]) + dot(m_rb[p], ub[p]) + dot(m_rk[p], v_st[p])
            y = y_st[0:c]
            for hh in range(1, nh):
                y = y + y_st[hh * c:(hh + 1) * c]
            y_ref[rows, sls[p]] = y
        for p in pairs:
            l1 = lw[p].astype(BF16)
            r1 = lw[p] - l1.astype(F32)
            l2 = r1.astype(BF16)
            l3 = (r1 - l2.astype(F32)).astype(BF16)
            tot_col = _dot_tn(l1, ones_c) + _dot_tn(l2, ones_c) + _dot_tn(l3, ones_c)
            p_end = jnp.exp(cum[p][c - 1:c, :] - cum[p])
            h_scr[p] = (h[p] * jnp.exp(tot_col) + _dot_tn(stack(b[p] * p_end), ub[p])
                        + _dot_tn(stack(k[p] * p_end), v_st[p]))
        return carry

    lax.fori_loop(0, n_sub, sub_chunk, 0)

    @pl.when(tc == pl.num_programs(1) - 1)
    def _():
        hT_ref[...] = h_scr[...]


def _wkv_chunked(rwkv, wkv0, grp, hd):
    b, t = grp["b"], grp["t"]
    da = rwkv[0].shape[1]
    nh = LANES // hd
    n_pairs = da // LANES
    c = LANES // nh
    tt = min(t, 256)
    eye = jnp.eye(nh, dtype=F32)
    h0 = wkv0.transpose(0, 1, 3, 2).reshape(b, n_pairs, nh, hd, 1, hd) * eye[None, None, :, None, :, None]
    h0 = h0.reshape(b, n_pairs, LANES, LANES)
    seq_spec = pl.BlockSpec((None, tt, da), lambda i, j: (i, j, 0))
    st_spec = pl.BlockSpec((None, n_pairs, LANES, LANES), lambda i, j: (i, 0, 0, 0))
    y, h_fin = pl.pallas_call(
        functools.partial(_wkv_chunk_kernel, tt // c, n_pairs, c, hd), grid=(b, t // tt),
        in_specs=[seq_spec] * 6 + [st_spec], out_specs=[seq_spec, st_spec],
        out_shape=[jax.ShapeDtypeStruct((b, t, da), F32), jax.ShapeDtypeStruct((b, n_pairs, LANES, LANES), F32)],
        scratch_shapes=[pltpu.VMEM((n_pairs, LANES, LANES), F32)],
        compiler_params=_cparams("arbitrary", "arbitrary"), name="wkv_chunked",
    )(*[x.reshape(b, t, da) for x in rwkv], h0)
    h6 = h_fin.reshape(b, n_pairs, nh, hd, nh, hd)
    s_fin = jnp.stack([h6[:, :, h, :, h, :] for h in range(nh)], axis=2)
    return y.reshape(b * t, da), s_fin.reshape(b, n_pairs * nh, hd, hd).transpose(0, 1, 3, 2)


def _scan(rwkv, wkv0, grp, seg, diag):
    b, t = grp["b"], grp["t"]
    da = rwkv[0].shape[1]
    hd = wkv0.shape[-1]
    s0 = wkv0.transpose(0, 2, 1, 3).reshape(b, hd, da)
    tt = min(t, 256)
    seq_spec = pl.BlockSpec((None, tt, da), lambda i, j: (i, j, 0))
    st_spec = pl.BlockSpec((None, hd, da), lambda i, j: (i, 0, 0))
    y, s_fin = pl.pallas_call(
        functools.partial(_scan_kernel, tt, da // LANES), grid=(b, t // tt),
        in_specs=[seq_spec] * 6 + [st_spec, pl.BlockSpec((LANES, LANES), lambda i, j: (0, 0)),
                                   pl.BlockSpec((hd, LANES), lambda i, j: (0, 0))],
        out_specs=[seq_spec, st_spec],
        out_shape=[jax.ShapeDtypeStruct((b, t, da), F32), jax.ShapeDtypeStruct((b, hd, da), F32)],
        scratch_shapes=[pltpu.VMEM((hd, da), F32)],
        compiler_params=_cparams("arbitrary", "arbitrary"), name="wkv_scan",
    )(*[x.reshape(b, t, da) for x in rwkv], s0, seg, diag)
    return y.reshape(b * t, da), s_fin.reshape(b, hd, da // hd, hd).transpose(0, 2, 1, 3)


def _sb_prompt_kernel(tq, hd, scale, bias_ref, q_ref, k_ref, v_ref, tri_ref, o_ref):
    p, i = pl.program_id(1), pl.program_id(2)
    nh = LANES // hd
    lane = lax.broadcasted_iota(jnp.int32, (1, LANES), 1)
    hmasks = [(lane >= hh * hd) & (lane < (hh + 1) * hd) for hh in range(nh)]
    q = q_ref[...] * scale
    qhs = [jnp.where(hm, q, 0.0).astype(BF16) for hm in hmasks]
    biases = [bias_ref[p * nh + hh] for hh in range(nh)]
    tri = tri_ref[...]

    def key_block(j, carry, vis):
        acc, runs = carry
        rows = pl.ds(pl.multiple_of(j * tq, tq), tq)
        kb = k_ref[rows, :].astype(BF16)
        v = v_ref[rows, :]
        zs = [_dot_nt(qh, kb) + bias for qh, bias in zip(qhs, biases)]
        sps = [_softplus(z) for z in zs]
        lks = [-sp if vis is None else jnp.where(vis, -sp, 0.0) for sp in sps]
        cums = [_split_dot(lk, tri) for lk in lks]
        new_runs = []
        for hh in range(nh):
            wts = jnp.exp(zs[hh] - sps[hh] + cums[hh] + runs[hh])
            if vis is not None:
                wts = jnp.where(vis, wts, 0.0)
            vh = jnp.where(hmasks[hh], v, 0.0).astype(BF16)
            acc = acc + jnp.dot(wts.astype(BF16), vh, preferred_element_type=F32)
            new_runs.append(runs[hh] + jnp.sum(lks[hh], axis=1, keepdims=True))
        return acc, tuple(new_runs)

    qrow = lax.broadcasted_iota(jnp.int32, (tq, 1), 0)
    kcol = lax.broadcasted_iota(jnp.int32, (1, tq), 1)
    carry = (jnp.zeros((tq, LANES), F32), tuple(jnp.zeros((tq, 1), F32) for _ in range(nh)))
    carry = key_block(i, carry, kcol < qrow)
    carry = lax.fori_loop(0, i, lambda s, c: key_block(i - 1 - s, c, None), carry)
    o_ref[...] = carry[0]


def _sb_prompt(proj_main, grp, sb_bias, db, hd, q_off):
    b, t = grp["b"], grp["t"]
    n = proj_main.shape[0]
    tq = min(t, 256)
    nq = t // tq
    cb = q_off // LANES
    tri = (jnp.arange(tq)[:, None] > jnp.arange(tq)[None, :]).astype(BF16)
    return pl.pallas_call(
        functools.partial(_sb_prompt_kernel, tq, hd, hd ** -0.5),
        grid=(b, db // LANES, nq),
        in_specs=[pl.BlockSpec(memory_space=pltpu.SMEM),
                  pl.BlockSpec((tq, LANES), lambda bb, p, i: (bb * nq + i, cb + p)),
                  pl.BlockSpec((t, LANES), lambda bb, p, i: (bb, cb + db // LANES + p)),
                  pl.BlockSpec((t, LANES), lambda bb, p, i: (bb, cb + 2 * (db // LANES) + p)),
                  pl.BlockSpec((tq, tq), lambda bb, p, i: (0, 0))],
        out_specs=pl.BlockSpec((tq, LANES), lambda bb, p, i: (bb * nq + i, p)),
        out_shape=jax.ShapeDtypeStruct((n, db), F32),
        compiler_params=_cparams("arbitrary", "arbitrary", "arbitrary"), name="sb_prompt",
    )(sb_bias, proj_main, proj_main, proj_main, tri)


def _sb_sample_kernel(n_pg, n_heads, tp, page, scale, pt_ref, q_ref, knt_ref, vnt_ref, *rest):
    kc_refs, vc_refs = rest[:n_pg], rest[n_pg:2 * n_pg]
    bias_ref, tri_ref, o_ref, kn_scr, vn_scr, acc_scr, run_scr = rest[2 * n_pg:]
    j = pl.program_id(1)
    rows = n_heads * tp
    tri = tri_ref[...]
    qb = q_ref[...].astype(BF16)

    def blocks(kts, vts, vis):
        n = len(kts)
        zs = [lax.dot_general(qb, kt.astype(BF16), (((2,), (1,)), ((0,), (0,))), preferred_element_type=F32)
              .reshape(rows, page) * scale + bias_ref[...] for kt in kts]
        sps = [_softplus(z) for z in zs]
        lks = [-sp if vis is None else jnp.where(vis, -sp, 0.0) for sp in sps]
        cums = [_split_dot(lk, tri) for lk in lks]
        run = run_scr[...]
        acc = acc_scr[...]
        for g in range(n):
            wts = jnp.exp(zs[g] - sps[g] + cums[g] + run)
            if vis is not None:
                wts = jnp.where(vis, wts, 0.0)
            acc = acc + lax.dot_general(wts.astype(BF16).reshape(n_heads, tp, page), vts[g].astype(BF16),
                                        (((2,), (2,)), ((0,), (0,))), preferred_element_type=F32)
            run = run + jnp.sum(lks[g], axis=1, keepdims=True)
        run_scr[...] = run
        acc_scr[...] = acc

    @pl.when(j == 0)
    def _():
        kn_scr[...] = jnp.zeros_like(kn_scr)
        vn_scr[...] = jnp.zeros_like(vn_scr)
        kn_scr[:, :, 0:tp] = knt_ref[...]
        vn_scr[:, :, 0:tp] = vnt_ref[...]
        acc_scr[...] = jnp.zeros_like(acc_scr)
        run_scr[...] = jnp.zeros_like(run_scr)
        s_idx = lax.broadcasted_iota(jnp.int32, (rows, page), 1)
        t_idx = lax.rem(lax.broadcasted_iota(jnp.int32, (rows, page), 0), tp)
        blocks([kn_scr[...]], [vn_scr[...]], s_idx < t_idx)

    blocks([ref[...] for ref in kc_refs], [ref[...] for ref in vc_refs], None)

    @pl.when(j == pl.num_programs(1) - 1)
    def _():
        o_ref[...] = acc_scr[...]


def _sb_sample(proj_main, grp, cache_kt, cache_vt, layer, page_table, sb_bias, db, hd, q_off):
    b, t = grp["b"], grp["t"]
    n_heads = db // hd
    page = cache_kt.shape[-1]
    n_pages = page_table.shape[1]
    tp = SUBLANES
    assert t <= tp
    n_pg = 4 if n_pages % 4 == 0 else (2 if n_pages % 2 == 0 else 1)
    col = lambda i: lax.slice_in_dim(proj_main, q_off + i * db, q_off + (i + 1) * db, axis=1).reshape(b, t, n_heads, hd)
    q = jnp.pad(col(0).transpose(0, 2, 1, 3), ((0, 0), (0, 0), (0, tp - t), (0, 0)))
    new_t = lambda x: jnp.pad(x.transpose(0, 2, 3, 1), ((0, 0), (0, 0), (0, 0), (0, tp - t)))
    bias_b = jnp.broadcast_to(jnp.repeat(sb_bias, tp)[:, None], (n_heads * tp, page))
    tri = (jnp.arange(page)[:, None] > jnp.arange(page)[None, :]).astype(BF16)
    q_spec = pl.BlockSpec((None, n_heads, tp, hd), lambda bb, j, pt: (bb, 0, 0, 0))
    new_spec = pl.BlockSpec((None, n_heads, hd, tp), lambda bb, j, pt: (bb, 0, 0, 0))

    def page_spec(g):
        return pl.BlockSpec((None, None, n_heads, hd, page),
                            lambda bb, j, pt: (layer, pt[bb * n_pages + n_pages - 1 - (j * n_pg + g)], 0, 0, 0))

    const = lambda shape: pl.BlockSpec(shape, lambda bb, j, pt: (0,) * len(shape))
    out = pl.pallas_call(
        functools.partial(_sb_sample_kernel, n_pg, n_heads, tp, page, hd ** -0.5),
        grid_spec=pltpu.PrefetchScalarGridSpec(
            num_scalar_prefetch=1, grid=(b, n_pages // n_pg),
            in_specs=[q_spec, new_spec, new_spec] + [page_spec(g) for g in range(n_pg)] * 2
                     + [const((n_heads * tp, page)), const((page, page))],
            out_specs=q_spec,
            scratch_shapes=[pltpu.VMEM((n_heads, hd, page), F32), pltpu.VMEM((n_heads, hd, page), F32),
                            pltpu.VMEM((n_heads, tp, hd), F32), pltpu.VMEM((n_heads * tp, 1), F32)]),
        out_shape=jax.ShapeDtypeStruct((b, n_heads, tp, hd), F32),
        compiler_params=_cparams("arbitrary", "arbitrary"), name="sb_sample",
    )(page_table.reshape(-1), q, new_t(col(1)), new_t(col(2)), *([cache_kt] * n_pg), *([cache_vt] * n_pg),
      bias_b, tri)
    return out[:, :, :t, :].transpose(0, 2, 1, 3).reshape(b * t, db)


def _merge_kernel(y_ref, r_ref, k_ref, v_ref, og_ref, yb_ref, ga_ref, gb_ref, wa_ref, wb_ref,
                  rk_ref, lg_ref, lb_ref, seg_ref, o_ref, ya_scr):
    @pl.when(pl.program_id(1) == 0)
    def _():
        seg = seg_ref[...]
        y = y_ref[...]
        hd_inv = 1.0 / jnp.sum(seg[0:1, :].astype(F32))
        mu = _head_sum(y, seg) * hd_inv
        yc = y - mu
        var = _head_sum(yc * yc, seg) * hd_inv
        yn = yc * lax.rsqrt(var + GN_EPS) * lg_ref[...] + lb_ref[...]
        bonus = _head_sum(r_ref[...] * k_ref[...] * rk_ref[...], seg)
        ya_scr[...] = ((yn + bonus * v_ref[...]) * og_ref[...]).astype(BF16)
    ma = jnp.dot(ya_scr[...], wa_ref[...], preferred_element_type=F32)
    mb = jnp.dot(yb_ref[...].astype(BF16), wb_ref[...], preferred_element_type=F32)
    o_ref[...] = (_sigmoid(ga_ref[...]) * ma + _sigmoid(gb_ref[...]) * mb).astype(BF16)


def _merge(y, rwkv_r, rwkv_k, rwkv_v, og, y_b, proj_main, gate_off, wa_bf, wb_bf, prm, seg, tm):
    n, da = y.shape
    db = y_b.shape[1]
    d = wa_bf.shape[1]
    tn = _tile(d, 512)
    ga_blk = gate_off // tn
    row = lambda w: pl.BlockSpec((tm, w), lambda i, j: (i, 0))
    vec = lambda w: pl.BlockSpec((1, w), lambda i, j: (0, 0))
    return pl.pallas_call(
        _merge_kernel, grid=(n // tm, d // tn),
        in_specs=[row(da)] * 5 + [row(db),
                  pl.BlockSpec((tm, tn), lambda i, j: (i, ga_blk + j)),
                  pl.BlockSpec((tm, tn), lambda i, j: (i, ga_blk + d // tn + j)),
                  pl.BlockSpec((da, tn), lambda i, j: (0, j)), pl.BlockSpec((db, tn), lambda i, j: (0, j)),
                  vec(da), vec(da), vec(da), pl.BlockSpec((LANES, LANES), lambda i, j: (0, 0))],
        out_specs=pl.BlockSpec((tm, tn), lambda i, j: (i, j)),
        out_shape=jax.ShapeDtypeStruct((n, d), BF16),
        scratch_shapes=[pltpu.VMEM((tm, da), BF16)],
        compiler_params=_cparams("arbitrary", "arbitrary"), name="merge",
    )(y, rwkv_r, rwkv_k, rwkv_v, og, y_b, proj_main, proj_main, wa_bf, wb_bf,
      prm["r_k"], prm["lnx_g"], prm["lnx_b"], seg)


def _wo_ln_kernel(alpha, m_ref, w_ref, x_ref, gm_ref, shf_ref, scf_ref, lg_ref, lb_ref, x1_o, h2_o):
    y = jnp.dot(m_ref[...], w_ref[...], preferred_element_type=F32)
    x1 = _std(alpha * x_ref[...] + gm_ref[...] * y, LN_EPS) * lg_ref[...] + lb_ref[...]
    x1_o[...] = x1
    h2_o[...] = (_std(x1, LN_EPS) * (1.0 + scf_ref[...]) + shf_ref[...]).astype(BF16)


def _wo_ln(merged, x, grp, mod, wo_bf, ln_g, ln_b, alpha, tm):
    n, d = x.shape
    row = pl.BlockSpec((tm, d), lambda i: (i, 0))
    vec = pl.BlockSpec((1, d), lambda i: (0, 0))
    return pl.pallas_call(
        functools.partial(_wo_ln_kernel, alpha), grid=(n // tm,),
        in_specs=[row, pl.BlockSpec((d, d), lambda i: (0, 0)), row,
                  _mod_spec(grp, 2, tm), _mod_spec(grp, 3, tm), _mod_spec(grp, 4, tm), vec, vec],
        out_specs=[row, row],
        out_shape=[jax.ShapeDtypeStruct((n, d), F32), jax.ShapeDtypeStruct((n, d), BF16)],
        compiler_params=_cparams("arbitrary"), name="wo_ln",
    )(merged, wo_bf, x, mod, mod, mod, ln_g, ln_b)


def _peer_q_kernel(ng, hk, h_ref, wq_ref, keys_ref, o_ref):
    qp = jnp.dot(h_ref[...], wq_ref[...], preferred_element_type=F32)
    for g in range(ng):
        o_ref[g] = _dot_nt(keys_ref[g], qp[:, g * hk:(g + 1) * hk].astype(BF16))


def _peer_q(h2, wq_bf, keys_bf, tm):
    n, d = h2.shape
    n_grp, nk, hk = keys_bf.shape
    ng = 4 if n_grp % 4 == 0 else 2
    return pl.pallas_call(
        functools.partial(_peer_q_kernel, ng, hk), grid=(n // tm, n_grp // ng),
        in_specs=[pl.BlockSpec((tm, d), lambda i, j: (i, 0)),
                  pl.BlockSpec((d, ng * hk), lambda i, j: (0, j)),
                  pl.BlockSpec((ng, nk, hk), lambda i, j: (j, 0, 0))],
        out_specs=pl.BlockSpec((ng, nk, tm), lambda i, j: (j, 0, i)),
        out_shape=jax.ShapeDtypeStruct((n_grp, nk, n), F32),
        compiler_params=_cparams("arbitrary", "arbitrary"), name="peer_q",
    )(h2, wq_bf, keys_bf)


def _top_rows(s, k, payloads=(), rid=None):
    if rid is None:
        rid = lax.broadcasted_iota(jnp.int32, s.shape, 0).astype(F32)
    vals, idxs, picked = [], [], [[] for _ in payloads]
    for _ in range(k):
        m = jnp.max(s, axis=0, keepdims=True)
        idx = jnp.min(jnp.where(s == m, rid, RID_NONE), axis=0, keepdims=True)
        hit = rid == idx
        vals.append(m)
        idxs.append(idx)
        for out, pay in zip(picked, payloads):
            out.append(jnp.max(jnp.where(hit, pay, -1.0), axis=0, keepdims=True))
        s = jnp.where(hit, -jnp.inf, s)
    return vals, idxs, picked


def _peer_topk_kernel(s_ref, g_o, i1_o, i2_o):
    tl = s_ref.shape[2]
    v0, i0, _ = _top_rows(s_ref[0], TOPK)
    v1, i1, _ = _top_rows(s_ref[1], TOPK)
    v0c, i0c = jnp.concatenate(v0, axis=0), jnp.concatenate(i0, axis=0)
    v1c, i1c = jnp.concatenate(v1, axis=0), jnp.concatenate(i1, axis=0)
    sub = lax.broadcasted_iota(jnp.int32, (SUBLANES, tl), 0)
    cand, rid, c1, c2 = [], [], [], []
    for i in range(TOPK // 2):
        nj = TOPK // (i + 1)
        for j0 in range(0, nj, SUBLANES):
            jj = sub + j0
            cand.append(jnp.where(jj < nj, v0[i] + v1c[j0:j0 + SUBLANES], -jnp.inf))
            rid.append((jj + i * TOPK).astype(F32))
            c1.append(jnp.broadcast_to(i0[i], (SUBLANES, tl)))
            c2.append(i1c[j0:j0 + SUBLANES])
    for i0_ in range(TOPK // 2, TOPK, SUBLANES):
        cand.append(v0c[i0_:i0_ + SUBLANES] + v1[0])
        rid.append(((sub + i0_) * TOPK).astype(F32))
        c1.append(i0c[i0_:i0_ + SUBLANES])
        c2.append(jnp.broadcast_to(i1[0], (SUBLANES, tl)))
    cat = lambda xs: jnp.concatenate(xs, axis=0)
    best, _, (e1, e2) = _top_rows(cat(cand), TOPK, (cat(c1), cat(c2)), rid=cat(rid))
    best = jnp.concatenate(best, axis=0)
    e = jnp.exp(best - best[0:1])
    g_o[...] = e / jnp.sum(e, axis=0, keepdims=True)
    i1_o[...] = jnp.concatenate(e1, axis=0)
    i2_o[...] = jnp.concatenate(e2, axis=0)


def _peer_topk(scores_t):
    n_grp, nk, n = scores_t.shape
    n_heads = n_grp // 2
    tl = _tile(n, LANES)
    out = jax.ShapeDtypeStruct((n_heads * TOPK, n), F32)
    ospec = pl.BlockSpec((TOPK, tl), lambda i, h: (h, i))
    return pl.pallas_call(
        _peer_topk_kernel, grid=(n // tl, n_heads),
        in_specs=[pl.BlockSpec((2, nk, tl), lambda i, h: (h, 0, i))],
        out_specs=[ospec] * 3, out_shape=[out] * 3,
        compiler_params=_cparams("arbitrary", "arbitrary"), name="peer_topk",
    )(scores_t)


def _peer_gate_kernel(tg, nk, g_ref, i1_ref, i2_ref, o_ref, g_s, i1_s, i2_s, t_scr):
    g_s[...] = g_ref[...].T
    i1_s[...] = i1_ref[...].T
    i2_s[...] = i2_ref[...].T
    m = g_s.shape[1]
    sub = lax.broadcasted_iota(jnp.int32, (nk, m), 0).astype(F32)
    steps = BF16_ROWS

    def body(nb, carry):
        rows = pl.ds(pl.multiple_of(nb * steps, steps), steps)
        g8, a8, b8 = g_s[rows, :], i1_s[rows, :], i2_s[rows, :]
        for i in range(steps):
            a_t = jnp.where(a8[i:i + 1, :] == sub, g8[i:i + 1, :], 0.0).astype(BF16)
            b_t = jnp.where(b8[i:i + 1, :] == sub, 1.0, 0.0).astype(BF16)
            t_scr[pl.ds(i, nk, stride=steps), :] = _dot_nt(a_t, b_t)
        for c in range(nk):
            o_ref[rows, c * nk:(c + 1) * nk] = t_scr[c * steps:(c + 1) * steps, :].astype(BF16)
        return carry

    lax.fori_loop(0, tg // steps, body, 0)


def _peer_gate(gate_t, i1_t, i2_t, nk):
    m, n = gate_t.shape
    tg = _tile(n, LANES)
    ispec = pl.BlockSpec((m, tg), lambda i: (0, i))
    return pl.pallas_call(
        functools.partial(_peer_gate_kernel, tg, nk), grid=(n // tg,),
        in_specs=[ispec] * 3,
        out_specs=pl.BlockSpec((tg, nk * nk), lambda i: (i, 0)),
        out_shape=jax.ShapeDtypeStruct((n, nk * nk), BF16),
        scratch_shapes=[pltpu.VMEM((tg, m), F32)] * 3 + [pltpu.VMEM((nk * BF16_ROWS, nk), F32)],
        compiler_params=_cparams("arbitrary"), name="peer_gate",
    )(gate_t, i1_t, i2_t)


def _gelu_tanh(x):
    return 0.5 * x * (1.0 + jnp.tanh(math.sqrt(2.0 / math.pi) * (x + 0.044715 * (x * x * x))))


def _peer_dense_kernel(alpha, h_ref, g_ref, ut_ref, v_ref, x1_ref, gf_ref, lg_ref, lb_ref, o_ref):
    c = pl.program_id(1)

    @pl.when(c == 0)
    def _():
        o_ref[...] = jnp.zeros_like(o_ref)

    act = _gelu_tanh(jnp.dot(h_ref[...], ut_ref[...], preferred_element_type=F32))
    p = (act * g_ref[...].astype(F32)).astype(BF16)
    o_ref[...] += jnp.dot(p, v_ref[...], preferred_element_type=F32)

    @pl.when(c == pl.num_programs(1) - 1)
    def _():
        z = alpha * x1_ref[...] + gf_ref[...] * o_ref[...]
        o_ref[...] = _std(z, LN_EPS) * lg_ref[...] + lb_ref[...]


def _peer_dense(h2, gmat, ut_bf, v_bf, x1, grp, mod, ln_g, ln_b, alpha, tm):
    n, d = x1.shape
    e = v_bf.shape[0]
    te = _tile(e, 512)
    row = pl.BlockSpec((tm, d), lambda i, c: (i, 0))
    vec = pl.BlockSpec((1, d), lambda i, c: (0, 0))
    return pl.pallas_call(
        functools.partial(_peer_dense_kernel, alpha), grid=(n // tm, e // te),
        in_specs=[row, pl.BlockSpec((tm, te), lambda i, c: (i, c)),
                  pl.BlockSpec((d, te), lambda i, c: (0, c)), pl.BlockSpec((te, d), lambda i, c: (c, 0)),
                  row, _mod_spec(grp, 5, tm), vec, vec],
        out_specs=row, out_shape=jax.ShapeDtypeStruct((n, d), F32),
        compiler_params=_cparams("arbitrary", "arbitrary"), name="peer_dense",
    )(h2, gmat, ut_bf, v_bf, x1, mod, ln_g, ln_b)


def _run_trunk(x, grp, mods, shift0, wkv0, cache_k, cache_v, page_table, w, dims):
    d, da, db, hd, lp, depth = dims["d"], dims["da"], dims["db"], dims["hd"], dims["lp"], dims["depth"]
    b, t = grp["b"], grp["t"]
    n = b * t
    tm = grp["tm"]
    alpha = (2 * depth) ** 0.25
    n_raw = 3 * da + dims["n_lora"]
    seg, diag = w["seg"], w["diag"]
    k_rows, v_rows, wkv_rows, shift_rows = [], [], [], []
    for l in range(depth):
        mod = mods[l]
        lw = w["layers"][l]
        proj_main = _proj(x, grp, mod, lw["w_main"], _tile(da, 512))
        proj_lora = _proj(x, grp, mod, lw["w_lora"], lp)
        if grp["per_row"]:
            prev_main = jnp.repeat(shift0[l][:, :3 * da], t, axis=0)
            prev_lora = jnp.repeat(jnp.pad(shift0[l][:, 3 * da:], ((0, 0), (0, lp - dims["n_lora"]))), t, axis=0)
        else:
            prev_main = shift0[l][:, None, :3 * da]
            prev_lora = jnp.pad(shift0[l][:, None, 3 * da:], ((0, 0), (0, 0), (0, lp - dims["n_lora"])))
        r_, w_, k_, v_, a_, b_, og = _rwkv_prep(proj_main, proj_lora, grp, prev_main, prev_lora, lw, seg)
        if t % (LANES // (LANES // hd)) == 0:
            y, s_fin = _wkv_chunked((r_, w_, k_, v_, a_, b_), wkv0[l], grp, hd)
        else:
            y, s_fin = _scan((r_, w_, k_, v_, a_, b_), wkv0[l], grp, seg, diag)
        if cache_k is None:
            y_b = _sb_prompt(proj_main, grp, lw["sb_bias"], db, hd, 3 * da)
        else:
            y_b = _sb_sample(proj_main, grp, cache_k, cache_v, l, page_table, lw["sb_bias"], db, hd, 3 * da)
        merged = _merge(y, r_, k_, v_, og, y_b, proj_main, 3 * da + 3 * db, lw["w_br_a"], lw["w_br_b"], lw, seg,
                        grp["tm_prep"])
        x1, h2 = _wo_ln(merged, x, grp, mod, lw["wo"], lw["ln1_g"], lw["ln1_b"], alpha, grp["tm_prep"])
        scores_t = _peer_q(h2, lw["wq"], lw["keys"], tm)
        gate_t, i1_t, i2_t = _peer_topk(scores_t)
        nk = lw["keys"].shape[1]
        gmat = _peer_gate(gate_t, i1_t, i2_t, nk)
        x = _peer_dense(h2, gmat, lw["ut"], lw["v"], x1, grp, mod, lw["ln2_g"], lw["ln2_b"], alpha, tm)
        p3 = proj_main.reshape(b, t, -1)
        k_rows.append(p3[:, :, 3 * da + db:3 * da + 2 * db].reshape(b, t, db // hd, hd))
        v_rows.append(p3[:, :, 3 * da + 2 * db:3 * da + 3 * db].reshape(b, t, db // hd, hd))
        wkv_rows.append(s_fin)
        shift_rows.append(jnp.concatenate(
            [p3[:, -1, :3 * da], proj_lora.reshape(b, t, lp)[:, -1, :dims["n_lora"]]], axis=-1))
        assert shift_rows[-1].shape[-1] == n_raw
    return (x.reshape(b, t, d), jnp.stack(k_rows), jnp.stack(v_rows), jnp.stack(wkv_rows), jnp.stack(shift_rows))


def kernel(x_prompt, x_sample, cache_k, cache_v, state_wkv, state_shift, page_table, c_prompt, c_sample,
           w_ada, b_ada, w_in, mu_shift, decay_bias, decay_up, aaa_bias, aaa_up, og_up, k_k, k_a, r_k,
           lnx_g, lnx_b, sb_bias, w_branch_a, w_branch_b, w_o, ln1_g, ln1_b, peer_wq, peer_keys, peer_u, peer_v,
           ln2_g, ln2_b):
    depth, d, _ = w_ada.shape
    hd = cache_k.shape[-1]
    da = decay_up.shape[2]
    db = w_branch_b.shape[1]
    wl, al, gl = decay_up.shape[1], aaa_up.shape[1], og_up.shape[1]
    n_lora = wl + al + gl
    lp = -(-n_lora // LANES) * LANES
    assert da % LANES == 0 and db % LANES == 0 and LANES % hd == 0
    dims = dict(d=d, da=da, db=db, hd=hd, lp=lp, depth=depth, n_lora=n_lora)

    seg = (jnp.arange(LANES)[:, None] // hd == jnp.arange(LANES)[None, :] // hd).astype(BF16)
    diag = (jnp.arange(hd)[:, None] == jnp.arange(LANES)[None, :] % hd).astype(F32)
    layers = []
    for l in range(depth):
        rw = 3 * da + n_lora
        pad_rows = lambda m, off: jnp.pad(m, ((off, lp - off - m.shape[0]), (0, 0))).astype(BF16)
        n_grp = peer_keys.shape[1] * peer_keys.shape[2]
        layers.append(dict(
            da=da, lp=lp,
            w_main=jnp.concatenate([w_in[l][:, :3 * da], w_in[l][:, rw:]], axis=1).astype(BF16),
            w_lora=jnp.pad(w_in[l][:, 3 * da:rw], ((0, 0), (0, lp - n_lora))).astype(BF16),
            mu_main=mu_shift[l][None, :3 * da],
            mu_lora=jnp.pad(mu_shift[l][None, 3 * da:], ((0, 0), (0, lp - n_lora))),
            decay_bias=decay_bias[l][None], aaa_bias=aaa_bias[l][None],
            wd=pad_rows(decay_up[l], 0), wa=pad_rows(aaa_up[l], wl), wg=pad_rows(og_up[l], wl + al),
            w_br_a=w_branch_a[l].astype(BF16), w_br_b=w_branch_b[l].astype(BF16),
            k_k=k_k[l][None], k_a=k_a[l][None], r_k=r_k[l].reshape(1, da),
            lnx_g=lnx_g[l][None], lnx_b=lnx_b[l][None], sb_bias=sb_bias[l],
            wo=w_o[l].astype(BF16), ln1_g=ln1_g[l][None], ln1_b=ln1_b[l][None],
            wq=peer_wq[l].astype(BF16), keys=peer_keys[l].reshape(n_grp, *peer_keys.shape[3:]).astype(BF16),
            ut=peer_u[l].T.astype(BF16), v=peer_v[l].astype(BF16),
            ln2_g=ln2_g[l][None], ln2_b=ln2_b[l][None]))
    w = dict(seg=seg, diag=diag, layers=layers)

    bp, tp_, _ = x_prompt.shape
    bs, ts, _ = x_sample.shape
    mod_all = _ada(jnp.concatenate([c_prompt, c_sample], axis=0), w_ada.astype(BF16), b_ada)

    def group(b, t):
        n = b * t
        tm = _tile(n, 512) if t < 512 else _tile(t, 512)
        tm_prep = _tile(n, 256) if t < 256 else _tile(t, 256)
        return dict(b=b, t=t, d=d, tm=tm, tm_prep=tm_prep, per_row=t < tm)

    gp, gs = group(bp, tp_), group(bs, ts)
    assert gp["per_row"] == (gp["t"] < gp["tm_prep"]) and gs["per_row"] == (gs["t"] < gs["tm_prep"])

    def mods_for(grp, lo, hi):
        out = []
        for l in range(depth):
            m = mod_all[l, lo:hi]
            out.append(jnp.repeat(m, grp["t"], axis=0) if grp["per_row"] else m.reshape(hi - lo, 6, 1, d))
        return out

    shift0_p = jnp.zeros((depth, bp, 3 * da + n_lora), F32)
    wkv0_p = jnp.zeros((depth, bp, da // hd, hd, hd), F32)
    y_p, k_p, v_p, wkv_p, shift_p = _run_trunk(
        x_prompt.reshape(bp * tp_, d), gp, mods_for(gp, 0, bp), shift0_p, wkv0_p, None, None, None, w, dims)
    y_s, k_s, v_s, wkv_s, shift_s = _run_trunk(
        x_sample.reshape(bs * ts, d), gs, mods_for(gs, bp, bp + bs), state_shift, state_wkv,
        cache_k.transpose(0, 1, 3, 4, 2), cache_v.transpose(0, 1, 3, 4, 2), page_table, w, dims)
    return (y_p, y_s, k_p, v_p, wkv_p, shift_p, k_s, v_s, wkv_s, shift_s)
```

```python
import functools
import math

import jax
import jax.numpy as jnp
from jax import lax
from jax.experimental import pallas as pl
from jax.experimental.pallas import tpu as pltpu

F32 = jnp.float32
BF16 = jnp.bfloat16
LN_EPS = 1e-5
GN_EPS = 64e-5
KK_EPS = 1e-12
TOPK = 16
LANES = 128
SUBLANES = 8
BF16_ROWS = 16
RID_NONE = 1e9
VMEM_LIMIT = 56 * 1024 * 1024


def _cparams(*sem):
    return pltpu.CompilerParams(dimension_semantics=sem, vmem_limit_bytes=VMEM_LIMIT)


def _tile(n, pref):
    if n <= pref:
        return n
    t = (pref // LANES) * LANES
    while t > LANES and n % t:
        t -= LANES
    assert n % t == 0, (n, pref)
    return t


def _std(x, eps):
    mu = jnp.mean(x, axis=-1, keepdims=True)
    xc = x - mu
    var = jnp.mean(xc * xc, axis=-1, keepdims=True)
    return xc * lax.rsqrt(var + eps)


def _sigmoid(x):
    return 1.0 / (1.0 + jnp.exp(-x))


def _softplus(x):
    return jnp.maximum(x, 0.0) + jnp.log(1.0 + jnp.exp(-jnp.abs(x)))


def _split_dot(x, m, *, left=False):
    hi = x.astype(BF16)
    lo = (x - hi.astype(F32)).astype(BF16)
    if left:
        return (jnp.dot(m, hi, preferred_element_type=F32) + jnp.dot(m, lo, preferred_element_type=F32))
    return (jnp.dot(hi, m, preferred_element_type=F32) + jnp.dot(lo, m, preferred_element_type=F32))


def _head_sum(x, seg):
    outs = [_split_dot(x[:, c * LANES:(c + 1) * LANES], seg) for c in range(x.shape[1] // LANES)]
    return outs[0] if len(outs) == 1 else jnp.concatenate(outs, axis=1)


def _dot_nt(a, b):
    return lax.dot_general(a, b, (((1,), (1,)), ((), ())), preferred_element_type=F32)


def _dot_tn(a, b):
    return lax.dot_general(a, b, (((0,), (0,)), ((), ())), preferred_element_type=F32)


def _ada_kernel(c_ref, w_ref, b_ref, o_ref):
    c = c_ref[...]
    s = (c * _sigmoid(c)).astype(BF16)
    o_ref[...] = jnp.dot(s, w_ref[...], preferred_element_type=F32) + b_ref[...]


def _ada(c_all, w_bf, b_ada):
    depth, d, d6 = w_bf.shape
    nb = c_all.shape[0]
    tn = _tile(d6, 1024)
    return pl.pallas_call(
        _ada_kernel, grid=(depth, d6 // tn),
        in_specs=[pl.BlockSpec((nb, d), lambda l, j: (0, 0)),
                  pl.BlockSpec((None, d, tn), lambda l, j: (l, 0, j)),
                  pl.BlockSpec((None, 1, tn), lambda l, j: (l, 0, j))],
        out_specs=pl.BlockSpec((None, nb, tn), lambda l, j: (l, 0, j)),
        out_shape=jax.ShapeDtypeStruct((depth, nb, d6), F32),
        compiler_params=_cparams("arbitrary", "arbitrary"), name="ada",
    )(c_all, w_bf, b_ada.reshape(depth, 1, d6))


def _mod_spec(grp, which, tm):
    d = grp["d"]
    if grp["per_row"]:
        return pl.BlockSpec((tm, d), lambda i, *_: (i, which))
    t = grp["t"]
    return pl.BlockSpec((None, None, 1, d), lambda i, *_: ((i * tm) // t, which, 0, 0))


def _proj_kernel(x_ref, sh_ref, sc_ref, w_ref, o_ref, h_scr):
    @pl.when(pl.program_id(1) == 0)
    def _():
        tm = x_ref.shape[0]
        rc = min(tm, 256)

        def chunk(c, carry):
            rows = pl.ds(pl.multiple_of(c * rc, rc), rc)
            vec = lambda ref: ref[...] if ref.shape[0] == 1 else ref[rows, :]
            h = _std(x_ref[rows, :], LN_EPS) * (1.0 + vec(sc_ref)) + vec(sh_ref)
            h_scr[rows, :] = h.astype(BF16)
            return carry

        lax.fori_loop(0, tm // rc, chunk, 0)
    o_ref[...] = jnp.dot(h_scr[...], w_ref[...], preferred_element_type=F32)


def _proj(x, grp, mod, w_bf, tn):
    n, d = x.shape
    p = w_bf.shape[1]
    tm = grp["tm_big"]
    return pl.pallas_call(
        _proj_kernel, grid=(n // tm, p // tn),
        in_specs=[pl.BlockSpec((tm, d), lambda i, j: (i, 0)),
                  _mod_spec(grp, 0, tm), _mod_spec(grp, 1, tm),
                  pl.BlockSpec((d, tn), lambda i, j: (0, j))],
        out_specs=pl.BlockSpec((tm, tn), lambda i, j: (i, j)),
        out_shape=jax.ShapeDtypeStruct((n, p), F32),
        scratch_shapes=[pltpu.VMEM((tm, d), BF16)],
        compiler_params=_cparams("arbitrary", "arbitrary"), name="proj",
    )(x, mod, mod, w_bf)


def _rwkv_prep_kernel(t_seq, tm, da, per_row, pm_ref, pl_ref, prevm_ref, prevl_ref, mum_ref, mul_ref,
                      dbias_ref, abias_ref, wd_ref, wa_ref, wg_ref, kk_ref, ka_ref, seg_ref,
                      r_o, w_o, k_o, v_o, a_o, b_o, og_o, carm, carl):
    i = pl.program_id(0)
    row = lax.broadcasted_iota(jnp.int32, (tm, 1), 0)

    def shifted(p, prev_ref, car):
        rolled = pltpu.roll(p, 1, 0)
        if per_row:
            return jnp.where(lax.rem(row, t_seq) == 0, prev_ref[...], rolled)
        first = jnp.where(lax.rem(i, t_seq // tm) == 0, prev_ref[...], car[...])
        car[...] = p[tm - 1:tm, :]
        return jnp.where(row == 0, first, rolled)

    p = pm_ref[...]
    pm = p + (shifted(p, prevm_ref, carm) - p) * mum_ref[...]
    q = pl_ref[...]
    lo = q + (shifted(q, prevl_ref, carl) - q) * mul_ref[...]
    r = pm[:, :da]
    k = pm[:, da:2 * da]
    v = pm[:, 2 * da:3 * da]
    w_pre = dbias_ref[...] + jnp.dot(jnp.tanh(lo).astype(BF16), wd_ref[...], preferred_element_type=F32)
    decay = jnp.exp(-math.exp(-0.5) * _sigmoid(w_pre))
    a = _sigmoid(abias_ref[...] + jnp.dot(lo.astype(BF16), wa_ref[...], preferred_element_type=F32))
    og = jnp.dot(_sigmoid(lo).astype(BF16), wg_ref[...], preferred_element_type=F32)
    kk = k * kk_ref[...]
    kk = kk * lax.rsqrt(_head_sum(kk * kk, seg_ref[...]) + KK_EPS)
    r_o[...] = r
    w_o[...] = decay
    k_o[...] = k * (1.0 + (a - 1.0) * ka_ref[...])
    v_o[...] = v
    a_o[...] = -kk
    b_o[...] = kk * a
    og_o[...] = og


def _rwkv_prep(proj_main, proj_lora, grp, prev_main, prev_lora, prm, seg):
    n = proj_main.shape[0]
    da, lp = prm["da"], prm["lp"]
    tm = grp["tm_prep"]
    per_row = grp["t"] < tm
    row_spec = lambda w: pl.BlockSpec((tm, w), lambda i: (i, 0))
    vec_spec = lambda w: pl.BlockSpec((1, w), lambda i: (0, 0))
    if per_row:
        prev_specs = [row_spec(3 * da), row_spec(lp)]
    else:
        t = grp["t"]
        prev_specs = [pl.BlockSpec((None, 1, 3 * da), lambda i: ((i * tm) // t, 0, 0)),
                      pl.BlockSpec((None, 1, lp), lambda i: ((i * tm) // t, 0, 0))]
    out = jax.ShapeDtypeStruct((n, da), F32)
    return pl.pallas_call(
        functools.partial(_rwkv_prep_kernel, grp["t"], tm, da, per_row), grid=(n // tm,),
        in_specs=[row_spec(3 * da), row_spec(lp)] + prev_specs + [
            vec_spec(3 * da), vec_spec(lp), vec_spec(da), vec_spec(da),
            pl.BlockSpec((lp, da), lambda i: (0, 0)), pl.BlockSpec((lp, da), lambda i: (0, 0)),
            pl.BlockSpec((lp, da), lambda i: (0, 0)), vec_spec(da), vec_spec(da),
            pl.BlockSpec((LANES, LANES), lambda i: (0, 0))],
        out_specs=[row_spec(da)] * 7, out_shape=[out] * 7,
        scratch_shapes=[pltpu.VMEM((1, 3 * da), F32), pltpu.VMEM((1, lp), F32)],
        compiler_params=_cparams("arbitrary"), name="rwkv_prep",
    )(proj_main, proj_lora, prev_main, prev_lora, prm["mu_main"], prm["mu_lora"], prm["decay_bias"],
      prm["aaa_bias"], prm["wd"], prm["wa"], prm["wg"], prm["k_k"], prm["k_a"], seg)


def _scan_kernel(tt, n_pairs, bb, r_ref, w_ref, k_ref, v_ref, a_ref, b_ref, s0_ref, seg_ref, dm_ref,
                 y_ref, sT_ref, s_scr):
    tc = pl.program_id(1)

    @pl.when(tc == 0)
    def _():
        s_scr[...] = s0_ref[...]

    seg = seg_ref[...]
    diag = dm_ref[...]
    steps = min(tt, SUBLANES)
    chains = [(s, p) for s in range(bb) for p in range(n_pairs)]
    lanes = lambda p: slice(p * LANES, (p + 1) * LANES)

    def block(tb, carry):
        rows = pl.ds(pl.multiple_of(tb * steps, steps), steps)
        r8, w8, k8, v8, a8, b8 = [[ref[s, rows, :] for s in range(bb)]
                                  for ref in (r_ref, w_ref, k_ref, v_ref, a_ref, b_ref)]
        ys = [[] for _ in range(bb)]
        for i in range(steps):
            row = lambda x, c: x[c[0]][i:i + 1, lanes(c[1])]
            st = [s_scr[s, :, lanes(p)] for s, p in chains]
            sa = [_split_dot(st[n] * row(a8, c), seg) for n, c in enumerate(chains)]
            vcol = [_split_dot(diag * row(v8, c), seg) for c in chains]
            st = [st[n] * row(w8, c) + sa[n] * row(b8, c) + vcol[n] * row(k8, c) for n, c in enumerate(chains)]
            for n, (s, p) in enumerate(chains):
                s_scr[s, :, lanes(p)] = st[n]
            yb = [_split_dot(st[n] * row(r8, c), seg) for n, c in enumerate(chains)]
            yv = [jnp.sum(x * diag, axis=0, keepdims=True) for x in yb]
            for s in range(bb):
                part = yv[s * n_pairs:(s + 1) * n_pairs]
                ys[s].append(part[0] if n_pairs == 1 else jnp.concatenate(part, axis=1))
        for s in range(bb):
            y_ref[s, rows, :] = jnp.concatenate(ys[s], axis=0)
        return carry

    lax.fori_loop(0, tt // steps, block, 0)

    @pl.when(tc == pl.num_programs(1) - 1)
    def _():
        sT_ref[...] = s_scr[...]


def _wkv_chunk_kernel(n_sub, n_pairs, c, hd, r_ref, w_ref, k_ref, v_ref, a_ref, b_ref, h0_ref, y_ref, hT_ref, h_scr):
    tc = pl.program_id(1)
    nh = LANES // hd
    rows_st = nh * c

    @pl.when(tc == 0)
    def _():
        h_scr[...] = h0_ref[...]

    lane_head = lax.broadcasted_iota(jnp.int32, (1, LANES), 1) // hd
    ri = lax.broadcasted_iota(jnp.int32, (rows_st, rows_st), 0)
    ci = lax.broadcasted_iota(jnp.int32, (rows_st, rows_st), 1)
    same_head = (ri // c) == (ci // c)
    strict = same_head & (ci < ri)
    incl = same_head & (ci <= ri)
    ti = lax.broadcasted_iota(jnp.int32, (c, c), 0)
    tj = lax.broadcasted_iota(jnp.int32, (c, c), 1)
    tri_incl = jnp.where(tj <= ti, 1.0, 0.0).astype(BF16)
    ones_c = jnp.ones((c, LANES), BF16)

    def stack(x):
        return jnp.concatenate([jnp.where(lane_head == h, x, 0.0) for h in range(nh)], axis=0).astype(BF16)

    def dot(a, b):
        return jnp.dot(a, b, preferred_element_type=F32)

    def sub_chunk(sc, carry):
        rows = pl.ds(pl.multiple_of(sc * c, c), c)
        pairs = range(n_pairs)
        sls = [slice(p * LANES, (p + 1) * LANES) for p in pairs]
        ld = lambda ref: [ref[rows, sl] for sl in sls]
        r, w, k, v, a, b = ld(r_ref), ld(w_ref), ld(k_ref), ld(v_ref), ld(a_ref), ld(b_ref)
        lw = [jnp.log(x) for x in w]
        cum = [_split_dot(x, tri_incl, left=True) for x in lw]
        p_inv = [jnp.exp(-x) for x in cum]
        a_st = [stack(a[p] * jnp.exp(cum[p] - lw[p])) for p in pairs]
        r_st = [stack(r[p] * jnp.exp(cum[p])) for p in pairs]
        b_st = [stack(b[p] * p_inv[p]) for p in pairs]
        k_st = [stack(k[p] * p_inv[p]) for p in pairs]
        v_st = [stack(x) for x in v]
        h = [h_scr[p] for p in pairs]
        hb = [x.astype(BF16) for x in h]
        npow = [jnp.where(strict, _dot_nt(a_st[p], b_st[p]), 0.0).astype(BF16) for p in pairs]
        n_ak = [jnp.where(strict, _dot_nt(a_st[p], k_st[p]), 0.0).astype(BF16) for p in pairs]
        u = [dot(a_st[p], hb[p]) + dot(n_ak[p], v_st[p]) for p in pairs]
        span = 1
        while span < c:
            u = [u[p] + dot(npow[p], u[p].astype(BF16)) for p in pairs]
            span *= 2
            if span < c:
                npow = [dot(x, x).astype(BF16) for x in npow]
        ub = [x.astype(BF16) for x in u]
        m_rb = [jnp.where(incl, _dot_nt(r_st[p], b_st[p]), 0.0).astype(BF16) for p in pairs]
        m_rk = [jnp.where(incl, _dot_nt(r_st[p], k_st[p]), 0.0).astype(BF16) for p in pairs]
        for p in pairs:
            y_st = dot(r_st[p], hb[p]) + dot(m_rb[p], ub[p]) + dot(m_rk[p], v_st[p])
            y = y_st[0:c]
            for hh in range(1, nh):
                y = y + y_st[hh * c:(hh + 1) * c]
            y_ref[rows, sls[p]] = y
        for p in pairs:
            l1 = lw[p].astype(BF16)
            r1 = lw[p] - l1.astype(F32)
            l2 = r1.astype(BF16)
            l3 = (r1 - l2.astype(F32)).astype(BF16)
            tot_col = _dot_tn(l1, ones_c) + _dot_tn(l2, ones_c) + _dot_tn(l3, ones_c)
            p_end = jnp.exp(cum[p][c - 1:c, :] - cum[p])
            h_scr[p] = (h[p] * jnp.exp(tot_col) + _dot_tn(stack(b[p] * p_end), ub[p])
                        + _dot_tn(stack(k[p] * p_end), v_st[p]))
        return carry

    lax.fori_loop(0, n_sub, sub_chunk, 0)

    @pl.when(tc == pl.num_programs(1) - 1)
    def _():
        hT_ref[...] = h_scr[...]


def _wkv_chunked(rwkv, wkv0, grp, hd):
    b, t = grp["b"], grp["t"]
    da = rwkv[0].shape[1]
    nh = LANES // hd
    n_pairs = da // LANES
    c = LANES // nh
    tt = min(t, 256)
    eye = jnp.eye(nh, dtype=F32)
    h0 = wkv0.transpose(0, 1, 3, 2).reshape(b, n_pairs, nh, hd, 1, hd) * eye[None, None, :, None, :, None]
    h0 = h0.reshape(b, n_pairs, LANES, LANES)
    seq_spec = pl.BlockSpec((None, tt, da), lambda i, j: (i, j, 0))
    st_spec = pl.BlockSpec((None, n_pairs, LANES, LANES), lambda i, j: (i, 0, 0, 0))
    y, h_fin = pl.pallas_call(
        functools.partial(_wkv_chunk_kernel, tt // c, n_pairs, c, hd), grid=(b, t // tt),
        in_specs=[seq_spec] * 6 + [st_spec], out_specs=[seq_spec, st_spec],
        out_shape=[jax.ShapeDtypeStruct((b, t, da), F32), jax.ShapeDtypeStruct((b, n_pairs, LANES, LANES), F32)],
        scratch_shapes=[pltpu.VMEM((n_pairs, LANES, LANES), F32)],
        compiler_params=_cparams("arbitrary", "arbitrary"), name="wkv_chunked",
    )(*[x.reshape(b, t, da) for x in rwkv], h0)
    h6 = h_fin.reshape(b, n_pairs, nh, hd, nh, hd)
    s_fin = jnp.stack([h6[:, :, h, :, h, :] for h in range(nh)], axis=2)
    return y.reshape(b * t, da), s_fin.reshape(b, n_pairs * nh, hd, hd).transpose(0, 1, 3, 2)


def _scan(rwkv, wkv0, grp, seg, diag):
    b, t = grp["b"], grp["t"]
    da = rwkv[0].shape[1]
    hd = wkv0.shape[-1]
    s0 = wkv0.transpose(0, 2, 1, 3).reshape(b, hd, da)
    tt = min(t, 256)
    bb = 4 if (t <= SUBLANES and b % 4 == 0) else 1
    seq_spec = pl.BlockSpec((bb, tt, da), lambda i, j: (i, j, 0))
    st_spec = pl.BlockSpec((bb, hd, da), lambda i, j: (i, 0, 0))
    y, s_fin = pl.pallas_call(
        functools.partial(_scan_kernel, tt, da // LANES, bb), grid=(b // bb, t // tt),
        in_specs=[seq_spec] * 6 + [st_spec, pl.BlockSpec((LANES, LANES), lambda i, j: (0, 0)),
                                   pl.BlockSpec((hd, LANES), lambda i, j: (0, 0))],
        out_specs=[seq_spec, st_spec],
        out_shape=[jax.ShapeDtypeStruct((b, t, da), F32), jax.ShapeDtypeStruct((b, hd, da), F32)],
        scratch_shapes=[pltpu.VMEM((bb, hd, da), F32)],
        compiler_params=_cparams("arbitrary", "arbitrary"), name="wkv_scan",
    )(*[x.reshape(b, t, da) for x in rwkv], s0, seg, diag)
    return y.reshape(b * t, da), s_fin.reshape(b, hd, da // hd, hd).transpose(0, 2, 1, 3)


def _sb_prompt_kernel(tq, hd, scale, bias_ref, q_ref, k_ref, v_ref, tri_ref, o_ref):
    p, i = pl.program_id(1), pl.program_id(2)
    nh = LANES // hd
    lane = lax.broadcasted_iota(jnp.int32, (1, LANES), 1)
    hmasks = [(lane >= hh * hd) & (lane < (hh + 1) * hd) for hh in range(nh)]
    q = q_ref[...] * scale
    qhs = [jnp.where(hm, q, 0.0).astype(BF16) for hm in hmasks]
    biases = [bias_ref[p * nh + hh] for hh in range(nh)]
    tri = tri_ref[...]

    def key_blocks(js, carry, vis):
        acc, runs = carry
        rows = [pl.ds(pl.multiple_of(j * tq, tq), tq) for j in js]
        kbs = [k_ref[r, :].astype(BF16) for r in rows]
        vs = [v_ref[r, :] for r in rows]
        chains = [(g, hh) for g in range(len(js)) for hh in range(nh)]
        zs = [_dot_nt(qhs[hh], kbs[g]) + biases[hh] for g, hh in chains]
        sps = [_softplus(z) for z in zs]
        lks = [-sp if vis is None else jnp.where(vis, -sp, 0.0) for sp in sps]
        cums = [_split_dot(lk, tri) for lk in lks]
        runs = list(runs)
        for n, (g, hh) in enumerate(chains):
            wts = jnp.exp(zs[n] - sps[n] + cums[n] + runs[hh])
            if vis is not None:
                wts = jnp.where(vis, wts, 0.0)
            vh = jnp.where(hmasks[hh], vs[g], 0.0).astype(BF16)
            acc = acc + jnp.dot(wts.astype(BF16), vh, preferred_element_type=F32)
            runs[hh] = runs[hh] + jnp.sum(lks[n], axis=1, keepdims=True)
        return acc, tuple(runs)

    qrow = lax.broadcasted_iota(jnp.int32, (tq, 1), 0)
    kcol = lax.broadcasted_iota(jnp.int32, (1, tq), 1)
    carry = (jnp.zeros((tq, LANES), F32), tuple(jnp.zeros((tq, 1), F32) for _ in range(nh)))
    carry = key_blocks([i], carry, kcol < qrow)
    carry = lax.fori_loop(0, i // 2, lambda s, c: key_blocks([i - 1 - 2 * s, i - 2 - 2 * s], c, None), carry)
    carry = lax.cond(lax.rem(i, 2) == 1, lambda c: key_blocks([0], c, None), lambda c: c, carry)
    o_ref[...] = carry[0]


def _sb_prompt(proj_main, grp, sb_bias, db, hd, q_off):
    b, t = grp["b"], grp["t"]
    n = proj_main.shape[0]
    tq = min(t, 256)
    nq = t // tq
    cb = q_off // LANES
    tri = (jnp.arange(tq)[:, None] > jnp.arange(tq)[None, :]).astype(BF16)
    return pl.pallas_call(
        functools.partial(_sb_prompt_kernel, tq, hd, hd ** -0.5),
        grid=(b, db // LANES, nq),
        in_specs=[pl.BlockSpec(memory_space=pltpu.SMEM),
                  pl.BlockSpec((tq, LANES), lambda bb, p, i: (bb * nq + i, cb + p)),
                  pl.BlockSpec((t, LANES), lambda bb, p, i: (bb, cb + db // LANES + p)),
                  pl.BlockSpec((t, LANES), lambda bb, p, i: (bb, cb + 2 * (db // LANES) + p)),
                  pl.BlockSpec((tq, tq), lambda bb, p, i: (0, 0))],
        out_specs=pl.BlockSpec((tq, LANES), lambda bb, p, i: (bb * nq + i, p)),
        out_shape=jax.ShapeDtypeStruct((n, db), F32),
        compiler_params=_cparams("arbitrary", "arbitrary", "arbitrary"), name="sb_prompt",
    )(sb_bias, proj_main, proj_main, proj_main, tri)


def _sb_sample_kernel(n_pg, n_heads, tp, page, scale, pt_ref, q_ref, knt_ref, vnt_ref, *rest):
    kc_refs, vc_refs = rest[:n_pg], rest[n_pg:2 * n_pg]
    bias_ref, tri_ref, o_ref, kn_scr, vn_scr, acc_scr, run_scr = rest[2 * n_pg:]
    j = pl.program_id(1)
    rows = n_heads * tp
    tri = tri_ref[...]
    qb = q_ref[...].astype(BF16)

    def blocks(kts, vts, vis):
        n = len(kts)
        zs = [lax.dot_general(qb, kt.astype(BF16), (((2,), (1,)), ((0,), (0,))), preferred_element_type=F32)
              .reshape(rows, page) * scale + bias_ref[...] for kt in kts]
        sps = [_softplus(z) for z in zs]
        lks = [-sp if vis is None else jnp.where(vis, -sp, 0.0) for sp in sps]
        cums = [_split_dot(lk, tri) for lk in lks]
        run = run_scr[...]
        acc = acc_scr[...]
        for g in range(n):
            wts = jnp.exp(zs[g] - sps[g] + cums[g] + run)
            if vis is not None:
                wts = jnp.where(vis, wts, 0.0)
            acc = acc + lax.dot_general(wts.astype(BF16).reshape(n_heads, tp, page), vts[g].astype(BF16),
                                        (((2,), (2,)), ((0,), (0,))), preferred_element_type=F32)
            run = run + jnp.sum(lks[g], axis=1, keepdims=True)
        run_scr[...] = run
        acc_scr[...] = acc

    @pl.when(j == 0)
    def _():
        kn_scr[...] = jnp.zeros_like(kn_scr)
        vn_scr[...] = jnp.zeros_like(vn_scr)
        kn_scr[:, :, 0:tp] = knt_ref[...]
        vn_scr[:, :, 0:tp] = vnt_ref[...]
        acc_scr[...] = jnp.zeros_like(acc_scr)
        run_scr[...] = jnp.zeros_like(run_scr)
        s_idx = lax.broadcasted_iota(jnp.int32, (rows, page), 1)
        t_idx = lax.rem(lax.broadcasted_iota(jnp.int32, (rows, page), 0), tp)
        blocks([kn_scr[...]], [vn_scr[...]], s_idx < t_idx)

    blocks([ref[...] for ref in kc_refs], [ref[...] for ref in vc_refs], None)

    @pl.when(j == pl.num_programs(1) - 1)
    def _():
        o_ref[...] = acc_scr[...]


def _sb_sample(proj_main, grp, cache_kt, cache_vt, layer, page_table, sb_bias, db, hd, q_off):
    b, t = grp["b"], grp["t"]
    n_heads = db // hd
    page = cache_kt.shape[-1]
    n_pages = page_table.shape[1]
    tp = SUBLANES
    assert t <= tp
    n_pg = 4 if n_pages % 4 == 0 else (2 if n_pages % 2 == 0 else 1)
    col = lambda i: lax.slice_in_dim(proj_main, q_off + i * db, q_off + (i + 1) * db, axis=1).reshape(b, t, n_heads, hd)
    q = jnp.pad(col(0).transpose(0, 2, 1, 3), ((0, 0), (0, 0), (0, tp - t), (0, 0)))
    new_t = lambda x: jnp.pad(x.transpose(0, 2, 3, 1), ((0, 0), (0, 0), (0, 0), (0, tp - t)))
    bias_b = jnp.broadcast_to(jnp.repeat(sb_bias, tp)[:, None], (n_heads * tp, page))
    tri = (jnp.arange(page)[:, None] > jnp.arange(page)[None, :]).astype(BF16)
    q_spec = pl.BlockSpec((None, n_heads, tp, hd), lambda bb, j, pt: (bb, 0, 0, 0))
    new_spec = pl.BlockSpec((None, n_heads, hd, tp), lambda bb, j, pt: (bb, 0, 0, 0))

    def page_spec(g):
        return pl.BlockSpec((None, None, n_heads, hd, page),
                            lambda bb, j, pt: (layer, pt[bb * n_pages + n_pages - 1 - (j * n_pg + g)], 0, 0, 0))

    const = lambda shape: pl.BlockSpec(shape, lambda bb, j, pt: (0,) * len(shape))
    out = pl.pallas_call(
        functools.partial(_sb_sample_kernel, n_pg, n_heads, tp, page, hd ** -0.5),
        grid_spec=pltpu.PrefetchScalarGridSpec(
            num_scalar_prefetch=1, grid=(b, n_pages // n_pg),
            in_specs=[q_spec, new_spec, new_spec] + [page_spec(g) for g in range(n_pg)] * 2
                     + [const((n_heads * tp, page)), const((page, page))],
            out_specs=q_spec,
            scratch_shapes=[pltpu.VMEM((n_heads, hd, page), F32), pltpu.VMEM((n_heads, hd, page), F32),
                            pltpu.VMEM((n_heads, tp, hd), F32), pltpu.VMEM((n_heads * tp, 1), F32)]),
        out_shape=jax.ShapeDtypeStruct((b, n_heads, tp, hd), F32),
        compiler_params=_cparams("arbitrary", "arbitrary"), name="sb_sample",
    )(page_table.reshape(-1), q, new_t(col(1)), new_t(col(2)), *([cache_kt] * n_pg), *([cache_vt] * n_pg),
      bias_b, tri)
    return out[:, :, :t, :].transpose(0, 2, 1, 3).reshape(b * t, db)


def _merge_kernel(y_ref, r_ref, k_ref, v_ref, og_ref, yb_ref, ga_ref, gb_ref, wa_ref, wb_ref,
                  rk_ref, lg_ref, lb_ref, seg_ref, o_ref, ya_scr):
    @pl.when(pl.program_id(1) == 0)
    def _():
        seg = seg_ref[...]
        y = y_ref[...]
        hd_inv = 1.0 / jnp.sum(seg[0:1, :].astype(F32))
        mu = _head_sum(y, seg) * hd_inv
        yc = y - mu
        var = _head_sum(yc * yc, seg) * hd_inv
        yn = yc * lax.rsqrt(var + GN_EPS) * lg_ref[...] + lb_ref[...]
        bonus = _head_sum(r_ref[...] * k_ref[...] * rk_ref[...], seg)
        ya_scr[...] = ((yn + bonus * v_ref[...]) * og_ref[...]).astype(BF16)
    ma = jnp.dot(ya_scr[...], wa_ref[...], preferred_element_type=F32)
    mb = jnp.dot(yb_ref[...].astype(BF16), wb_ref[...], preferred_element_type=F32)
    o_ref[...] = (_sigmoid(ga_ref[...]) * ma + _sigmoid(gb_ref[...]) * mb).astype(BF16)


def _merge(y, rwkv_r, rwkv_k, rwkv_v, og, y_b, proj_main, gate_off, wa_bf, wb_bf, prm, seg, tm):
    n, da = y.shape
    db = y_b.shape[1]
    d = wa_bf.shape[1]
    tn = _tile(d, 512)
    ga_blk = gate_off // tn
    row = lambda w: pl.BlockSpec((tm, w), lambda i, j: (i, 0))
    vec = lambda w: pl.BlockSpec((1, w), lambda i, j: (0, 0))
    return pl.pallas_call(
        _merge_kernel, grid=(n // tm, d // tn),
        in_specs=[row(da)] * 5 + [row(db),
                  pl.BlockSpec((tm, tn), lambda i, j: (i, ga_blk + j)),
                  pl.BlockSpec((tm, tn), lambda i, j: (i, ga_blk + d // tn + j)),
                  pl.BlockSpec((da, tn), lambda i, j: (0, j)), pl.BlockSpec((db, tn), lambda i, j: (0, j)),
                  vec(da), vec(da), vec(da), pl.BlockSpec((LANES, LANES), lambda i, j: (0, 0))],
        out_specs=pl.BlockSpec((tm, tn), lambda i, j: (i, j)),
        out_shape=jax.ShapeDtypeStruct((n, d), BF16),
        scratch_shapes=[pltpu.VMEM((tm, da), BF16)],
        compiler_params=_cparams("arbitrary", "arbitrary"), name="merge",
    )(y, rwkv_r, rwkv_k, rwkv_v, og, y_b, proj_main, proj_main, wa_bf, wb_bf,
      prm["r_k"], prm["lnx_g"], prm["lnx_b"], seg)


def _wo_ln_kernel(alpha, m_ref, w_ref, x_ref, gm_ref, shf_ref, scf_ref, lg_ref, lb_ref, x1_o, h2_o):
    y = jnp.dot(m_ref[...], w_ref[...], preferred_element_type=F32)
    x1 = _std(alpha * x_ref[...] + gm_ref[...] * y, LN_EPS) * lg_ref[...] + lb_ref[...]
    x1_o[...] = x1
    h2_o[...] = (_std(x1, LN_EPS) * (1.0 + scf_ref[...]) + shf_ref[...]).astype(BF16)


def _wo_ln(merged, x, grp, mod, wo_bf, ln_g, ln_b, alpha, tm):
    n, d = x.shape
    row = pl.BlockSpec((tm, d), lambda i: (i, 0))
    vec = pl.BlockSpec((1, d), lambda i: (0, 0))
    return pl.pallas_call(
        functools.partial(_wo_ln_kernel, alpha), grid=(n // tm,),
        in_specs=[row, pl.BlockSpec((d, d), lambda i: (0, 0)), row,
                  _mod_spec(grp, 2, tm), _mod_spec(grp, 3, tm), _mod_spec(grp, 4, tm), vec, vec],
        out_specs=[row, row],
        out_shape=[jax.ShapeDtypeStruct((n, d), F32), jax.ShapeDtypeStruct((n, d), BF16)],
        compiler_params=_cparams("arbitrary"), name="wo_ln",
    )(merged, wo_bf, x, mod, mod, mod, ln_g, ln_b)


def _peer_q_kernel(ng, hk, h_ref, wq_ref, keys_ref, o_ref):
    qp = jnp.dot(h_ref[...], wq_ref[...], preferred_element_type=F32)
    for g in range(ng):
        o_ref[g] = _dot_nt(keys_ref[g], qp[:, g * hk:(g + 1) * hk].astype(BF16))


def _peer_q(h2, wq_bf, keys_bf, tm):
    n, d = h2.shape
    n_grp, nk, hk = keys_bf.shape
    ng = 4 if n_grp % 4 == 0 else 2
    return pl.pallas_call(
        functools.partial(_peer_q_kernel, ng, hk), grid=(n // tm, n_grp // ng),
        in_specs=[pl.BlockSpec((tm, d), lambda i, j: (i, 0)),
                  pl.BlockSpec((d, ng * hk), lambda i, j: (0, j)),
                  pl.BlockSpec((ng, nk, hk), lambda i, j: (j, 0, 0))],
        out_specs=pl.BlockSpec((ng, nk, tm), lambda i, j: (j, 0, i)),
        out_shape=jax.ShapeDtypeStruct((n_grp, nk, n), F32),
        compiler_params=_cparams("arbitrary", "arbitrary"), name="peer_q",
    )(h2, wq_bf, keys_bf)


def _top_rows(s, k, payloads=(), rid=None):
    if rid is None:
        rid = lax.broadcasted_iota(jnp.int32, s.shape, 0).astype(F32)
    vals, idxs, picked = [], [], [[] for _ in payloads]
    for _ in range(k):
        m = jnp.max(s, axis=0, keepdims=True)
        idx = jnp.min(jnp.where(s == m, rid, RID_NONE), axis=0, keepdims=True)
        hit = rid == idx
        vals.append(m)
        idxs.append(idx)
        for out, pay in zip(picked, payloads):
            out.append(jnp.max(jnp.where(hit, pay, -1.0), axis=0, keepdims=True))
        s = jnp.where(hit, -jnp.inf, s)
    return vals, idxs, picked


def _peer_topk_kernel(s_ref, g_o, i1_o, i2_o):
    tl = s_ref.shape[2]
    v0, i0, _ = _top_rows(s_ref[0], TOPK)
    v1, i1, _ = _top_rows(s_ref[1], TOPK)
    v0c, i0c = jnp.concatenate(v0, axis=0), jnp.concatenate(i0, axis=0)
    v1c, i1c = jnp.concatenate(v1, axis=0), jnp.concatenate(i1, axis=0)
    sub = lax.broadcasted_iota(jnp.int32, (SUBLANES, tl), 0)
    cand, rid, c1, c2 = [], [], [], []
    for i in range(TOPK // 2):
        nj = TOPK // (i + 1)
        for j0 in range(0, nj, SUBLANES):
            jj = sub + j0
            cand.append(jnp.where(jj < nj, v0[i] + v1c[j0:j0 + SUBLANES], -jnp.inf))
            rid.append((jj + i * TOPK).astype(F32))
            c1.append(jnp.broadcast_to(i0[i], (SUBLANES, tl)))
            c2.append(i1c[j0:j0 + SUBLANES])
    for i0_ in range(TOPK // 2, TOPK, SUBLANES):
        cand.append(v0c[i0_:i0_ + SUBLANES] + v1[0])
        rid.append(((sub + i0_) * TOPK).astype(F32))
        c1.append(i0c[i0_:i0_ + SUBLANES])
        c2.append(jnp.broadcast_to(i1[0], (SUBLANES, tl)))
    cat = lambda xs: jnp.concatenate(xs, axis=0)
    best, _, (e1, e2) = _top_rows(cat(cand), TOPK, (cat(c1), cat(c2)), rid=cat(rid))
    best = jnp.concatenate(best, axis=0)
    e = jnp.exp(best - best[0:1])
    g_o[...] = e / jnp.sum(e, axis=0, keepdims=True)
    i1_o[...] = jnp.concatenate(e1, axis=0)
    i2_o[...] = jnp.concatenate(e2, axis=0)


def _peer_topk(scores_t):
    n_grp, nk, n = scores_t.shape
    n_heads = n_grp // 2
    tl = _tile(n, LANES)
    out = jax.ShapeDtypeStruct((n_heads * TOPK, n), F32)
    ospec = pl.BlockSpec((TOPK, tl), lambda i, h: (h, i))
    return pl.pallas_call(
        _peer_topk_kernel, grid=(n // tl, n_heads),
        in_specs=[pl.BlockSpec((2, nk, tl), lambda i, h: (h, 0, i))],
        out_specs=[ospec] * 3, out_shape=[out] * 3,
        compiler_params=_cparams("arbitrary", "arbitrary"), name="peer_topk",
    )(scores_t)


def _peer_gate_kernel(tg, nk, g_ref, i1_ref, i2_ref, o_ref, g_s, i1_s, i2_s, t_scr):
    g_s[...] = g_ref[...].T
    i1_s[...] = i1_ref[...].T
    i2_s[...] = i2_ref[...].T
    m = g_s.shape[1]
    sub = lax.broadcasted_iota(jnp.int32, (nk, m), 0).astype(F32)
    steps = BF16_ROWS

    def body(nb, carry):
        rows = pl.ds(pl.multiple_of(nb * steps, steps), steps)
        g8, a8, b8 = g_s[rows, :], i1_s[rows, :], i2_s[rows, :]
        a_t = [jnp.where(a8[i:i + 1, :] == sub, g8[i:i + 1, :], 0.0).astype(BF16) for i in range(steps)]
        b_t = [jnp.where(b8[i:i + 1, :] == sub, 1.0, 0.0).astype(BF16) for i in range(steps)]
        g_tok = [_dot_nt(a_t[i], b_t[i]) for i in range(steps)]
        for i in range(steps):
            t_scr[pl.ds(i, nk, stride=steps), :] = g_tok[i]
        for c in range(nk):
            o_ref[rows, c * nk:(c + 1) * nk] = t_scr[c * steps:(c + 1) * steps, :].astype(BF16)
        return carry

    lax.fori_loop(0, tg // steps, body, 0)


def _peer_gate(gate_t, i1_t, i2_t, nk):
    m, n = gate_t.shape
    tg = _tile(n, LANES)
    ispec = pl.BlockSpec((m, tg), lambda i: (0, i))
    return pl.pallas_call(
        functools.partial(_peer_gate_kernel, tg, nk), grid=(n // tg,),
        in_specs=[ispec] * 3,
        out_specs=pl.BlockSpec((tg, nk * nk), lambda i: (i, 0)),
        out_shape=jax.ShapeDtypeStruct((n, nk * nk), BF16),
        scratch_shapes=[pltpu.VMEM((tg, m), F32)] * 3 + [pltpu.VMEM((nk * BF16_ROWS, nk), F32)],
        compiler_params=_cparams("arbitrary"), name="peer_gate",
    )(gate_t, i1_t, i2_t)


def _gelu_tanh(x):
    return 0.5 * x * (1.0 + jnp.tanh(math.sqrt(2.0 / math.pi) * (x + 0.044715 * (x * x * x))))


def _peer_dense_kernel(alpha, h_ref, g_ref, ut_ref, v_ref, x1_ref, gf_ref, lg_ref, lb_ref, o_ref):
    c = pl.program_id(1)

    @pl.when(c == 0)
    def _():
        o_ref[...] = jnp.zeros_like(o_ref)

    act = _gelu_tanh(jnp.dot(h_ref[...], ut_ref[...], preferred_element_type=F32))
    p = (act * g_ref[...].astype(F32)).astype(BF16)
    o_ref[...] += jnp.dot(p, v_ref[...], preferred_element_type=F32)

    @pl.when(c == pl.num_programs(1) - 1)
    def _():
        z = alpha * x1_ref[...] + gf_ref[...] * o_ref[...]
        o_ref[...] = _std(z, LN_EPS) * lg_ref[...] + lb_ref[...]


def _peer_dense(h2, gmat, ut_bf, v_bf, x1, grp, mod, ln_g, ln_b, alpha, tm):
    n, d = x1.shape
    e = v_bf.shape[0]
    te = _tile(e, 1024)
    row = pl.BlockSpec((tm, d), lambda i, c: (i, 0))
    vec = pl.BlockSpec((1, d), lambda i, c: (0, 0))
    return pl.pallas_call(
        functools.partial(_peer_dense_kernel, alpha), grid=(n // tm, e // te),
        in_specs=[row, pl.BlockSpec((tm, te), lambda i, c: (i, c)),
                  pl.BlockSpec((d, te), lambda i, c: (0, c)), pl.BlockSpec((te, d), lambda i, c: (c, 0)),
                  row, _mod_spec(grp, 5, tm), vec, vec],
        out_specs=row, out_shape=jax.ShapeDtypeStruct((n, d), F32),
        compiler_params=_cparams("arbitrary", "arbitrary"), name="peer_dense",
    )(h2, gmat, ut_bf, v_bf, x1, mod, ln_g, ln_b)


def _run_trunk(x, grp, mods, shift0, wkv0, cache_k, cache_v, page_table, w, dims):
    d, da, db, hd, lp, depth = dims["d"], dims["da"], dims["db"], dims["hd"], dims["lp"], dims["depth"]
    b, t = grp["b"], grp["t"]
    n = b * t
    tm = grp["tm"]
    alpha = (2 * depth) ** 0.25
    n_raw = 3 * da + dims["n_lora"]
    seg, diag = w["seg"], w["diag"]
    k_rows, v_rows, wkv_rows, shift_rows = [], [], [], []
    for l in range(depth):
        mod = mods[l]
        lw = w["layers"][l]
        proj_main = _proj(x, grp, mod, lw["w_main"], _tile(da, 1024))
        proj_lora = _proj(x, grp, mod, lw["w_lora"], lp)
        if grp["per_row"]:
            prev_main = jnp.repeat(shift0[l][:, :3 * da], t, axis=0)
            prev_lora = jnp.repeat(jnp.pad(shift0[l][:, 3 * da:], ((0, 0), (0, lp - dims["n_lora"]))), t, axis=0)
        else:
            prev_main = shift0[l][:, None, :3 * da]
            prev_lora = jnp.pad(shift0[l][:, None, 3 * da:], ((0, 0), (0, 0), (0, lp - dims["n_lora"])))
        r_, w_, k_, v_, a_, b_, og = _rwkv_prep(proj_main, proj_lora, grp, prev_main, prev_lora, lw, seg)
        if t % (LANES // (LANES // hd)) == 0:
            y, s_fin = _wkv_chunked((r_, w_, k_, v_, a_, b_), wkv0[l], grp, hd)
        else:
            y, s_fin = _scan((r_, w_, k_, v_, a_, b_), wkv0[l], grp, seg, diag)
        if cache_k is None:
            y_b = _sb_prompt(proj_main, grp, lw["sb_bias"], db, hd, 3 * da)
        else:
            y_b = _sb_sample(proj_main, grp, cache_k, cache_v, l, page_table, lw["sb_bias"], db, hd, 3 * da)
        merged = _merge(y, r_, k_, v_, og, y_b, proj_main, 3 * da + 3 * db, lw["w_br_a"], lw["w_br_b"], lw, seg,
                        grp["tm"])
        x1, h2 = _wo_ln(merged, x, grp, mod, lw["wo"], lw["ln1_g"], lw["ln1_b"], alpha, grp["tm_prep"])
        scores_t = _peer_q(h2, lw["wq"], lw["keys"], tm)
        gate_t, i1_t, i2_t = _peer_topk(scores_t)
        nk = lw["keys"].shape[1]
        gmat = _peer_gate(gate_t, i1_t, i2_t, nk)
        x = _peer_dense(h2, gmat, lw["ut"], lw["v"], x1, grp, mod, lw["ln2_g"], lw["ln2_b"], alpha, tm)
        p3 = proj_main.reshape(b, t, -1)
        k_rows.append(p3[:, :, 3 * da + db:3 * da + 2 * db].reshape(b, t, db // hd, hd))
        v_rows.append(p3[:, :, 3 * da + 2 * db:3 * da + 3 * db].reshape(b, t, db // hd, hd))
        wkv_rows.append(s_fin)
        shift_rows.append(jnp.concatenate(
            [p3[:, -1, :3 * da], proj_lora.reshape(b, t, lp)[:, -1, :dims["n_lora"]]], axis=-1))
        assert shift_rows[-1].shape[-1] == n_raw
    return (x.reshape(b, t, d), jnp.stack(k_rows), jnp.stack(v_rows), jnp.stack(wkv_rows), jnp.stack(shift_rows))


def kernel(x_prompt, x_sample, cache_k, cache_v, state_wkv, state_shift, page_table, c_prompt, c_sample,
           w_ada, b_ada, w_in, mu_shift, decay_bias, decay_up, aaa_bias, aaa_up, og_up, k_k, k_a, r_k,
           lnx_g, lnx_b, sb_bias, w_branch_a, w_branch_b, w_o, ln1_g, ln1_b, peer_wq, peer_keys, peer_u, peer_v,
           ln2_g, ln2_b):
    depth, d, _ = w_ada.shape
    hd = cache_k.shape[-1]
    da = decay_up.shape[2]
    db = w_branch_b.shape[1]
    wl, al, gl = decay_up.shape[1], aaa_up.shape[1], og_up.shape[1]
    n_lora = wl + al + gl
    lp = -(-n_lora // LANES) * LANES
    assert da % LANES == 0 and db % LANES == 0 and LANES % hd == 0
    dims = dict(d=d, da=da, db=db, hd=hd, lp=lp, depth=depth, n_lora=n_lora)

    seg = (jnp.arange(LANES)[:, None] // hd == jnp.arange(LANES)[None, :] // hd).astype(BF16)
    diag = (jnp.arange(hd)[:, None] == jnp.arange(LANES)[None, :] % hd).astype(F32)
    layers = []
    for l in range(depth):
        rw = 3 * da + n_lora
        pad_rows = lambda m, off: jnp.pad(m, ((off, lp - off - m.shape[0]), (0, 0))).astype(BF16)
        n_grp = peer_keys.shape[1] * peer_keys.shape[2]
        layers.append(dict(
            da=da, lp=lp,
            w_main=jnp.concatenate([w_in[l][:, :3 * da], w_in[l][:, rw:]], axis=1).astype(BF16),
            w_lora=jnp.pad(w_in[l][:, 3 * da:rw], ((0, 0), (0, lp - n_lora))).astype(BF16),
            mu_main=mu_shift[l][None, :3 * da],
            mu_lora=jnp.pad(mu_shift[l][None, 3 * da:], ((0, 0), (0, lp - n_lora))),
            decay_bias=decay_bias[l][None], aaa_bias=aaa_bias[l][None],
            wd=pad_rows(decay_up[l], 0), wa=pad_rows(aaa_up[l], wl), wg=pad_rows(og_up[l], wl + al),
            w_br_a=w_branch_a[l].astype(BF16), w_br_b=w_branch_b[l].astype(BF16),
            k_k=k_k[l][None], k_a=k_a[l][None], r_k=r_k[l].reshape(1, da),
            lnx_g=lnx_g[l][None], lnx_b=lnx_b[l][None], sb_bias=sb_bias[l],
            wo=w_o[l].astype(BF16), ln1_g=ln1_g[l][None], ln1_b=ln1_b[l][None],
            wq=peer_wq[l].astype(BF16), keys=peer_keys[l].reshape(n_grp, *peer_keys.shape[3:]).astype(BF16),
            ut=peer_u[l].T.astype(BF16), v=peer_v[l].astype(BF16),
            ln2_g=ln2_g[l][None], ln2_b=ln2_b[l][None]))
    w = dict(seg=seg, diag=diag, layers=layers)

    bp, tp_, _ = x_prompt.shape
    bs, ts, _ = x_sample.shape
    mod_all = _ada(jnp.concatenate([c_prompt, c_sample], axis=0), w_ada.astype(BF16), b_ada)

    def group(b, t):
        n = b * t
        tm = _tile(n, 512) if t < 512 else _tile(t, 512)
        tm_prep = _tile(n, 256) if t < 256 else _tile(t, 256)
        tm_big = _tile(n, 1024) if t < tm else _tile(t, 1024)
        return dict(b=b, t=t, d=d, tm=tm, tm_prep=tm_prep, tm_big=tm_big, per_row=t < tm)

    gp, gs = group(bp, tp_), group(bs, ts)
    assert gp["per_row"] == (gp["t"] < gp["tm_prep"]) and gs["per_row"] == (gs["t"] < gs["tm_prep"])

    def mods_for(grp, lo, hi):
        out = []
        for l in range(depth):
            m = mod_all[l, lo:hi]
            out.append(jnp.repeat(m, grp["t"], axis=0) if grp["per_row"] else m.reshape(hi - lo, 6, 1, d))
        return out

    shift0_p = jnp.zeros((depth, bp, 3 * da + n_lora), F32)
    wkv0_p = jnp.zeros((depth, bp, da // hd, hd, hd), F32)
    y_p, k_p, v_p, wkv_p, shift_p = _run_trunk(
        x_prompt.reshape(bp * tp_, d), gp, mods_for(gp, 0, bp), shift0_p, wkv0_p, None, None, None, w, dims)
    y_s, k_s, v_s, wkv_s, shift_s = _run_trunk(
        x_sample.reshape(bs * ts, d), gs, mods_for(gs, bp, bp + bs), state_shift, state_wkv,
        cache_k.transpose(0, 1, 3, 4, 2), cache_v.transpose(0, 1, 3, 4, 2), page_table, w, dims)
    return (y_p, y_s, k_p, v_p, wkv_p, shift_p, k_s, v_s, wkv_s, shift_s)
```

```python
import functools
import math

import jax
import jax.numpy as jnp
from jax import lax
from jax.experimental import pallas as pl
from jax.experimental.pallas import tpu as pltpu

F32 = jnp.float32
BF16 = jnp.bfloat16
LN_EPS = 1e-5
GN_EPS = 64e-5
KK_EPS = 1e-12
TOPK = 16
LANES = 128
SUBLANES = 8
BF16_ROWS = 16
RID_NONE = 1e9
SB_UNROLL = 2
VMEM_LIMIT = 56 * 1024 * 1024


def _cparams(*sem):
    return pltpu.CompilerParams(dimension_semantics=sem, vmem_limit_bytes=VMEM_LIMIT)


def _tile(n, pref):
    if n <= pref:
        return n
    t = (pref // LANES) * LANES
    while t > LANES and n % t:
        t -= LANES
    assert n % t == 0, (n, pref)
    return t


def _std(x, eps):
    mu = jnp.mean(x, axis=-1, keepdims=True)
    xc = x - mu
    var = jnp.mean(xc * xc, axis=-1, keepdims=True)
    return xc * lax.rsqrt(var + eps)


def _sigmoid(x):
    return 1.0 / (1.0 + jnp.exp(-x))


def _softplus(x):
    return jnp.maximum(x, 0.0) + jnp.log(1.0 + jnp.exp(-jnp.abs(x)))


def _split_dot(x, m, *, left=False):
    hi = x.astype(BF16)
    lo = (x - hi.astype(F32)).astype(BF16)
    if left:
        return (jnp.dot(m, hi, preferred_element_type=F32) + jnp.dot(m, lo, preferred_element_type=F32))
    return (jnp.dot(hi, m, preferred_element_type=F32) + jnp.dot(lo, m, preferred_element_type=F32))


def _head_sum(x, seg):
    outs = [_split_dot(x[:, c * LANES:(c + 1) * LANES], seg) for c in range(x.shape[1] // LANES)]
    return outs[0] if len(outs) == 1 else jnp.concatenate(outs, axis=1)


def _dot_nt(a, b):
    return lax.dot_general(a, b, (((1,), (1,)), ((), ())), preferred_element_type=F32)


def _dot_tn(a, b):
    return lax.dot_general(a, b, (((0,), (0,)), ((), ())), preferred_element_type=F32)


def _ada_kernel(c_ref, w_ref, b_ref, o_ref):
    c = c_ref[...]
    s = (c * _sigmoid(c)).astype(BF16)
    o_ref[...] = jnp.dot(s, w_ref[...], preferred_element_type=F32) + b_ref[...]


def _ada(c_all, w_bf, b_ada):
    depth, d, d6 = w_bf.shape
    nb = c_all.shape[0]
    tn = _tile(d6, 1024)
    return pl.pallas_call(
        _ada_kernel, grid=(depth, d6 // tn),
        in_specs=[pl.BlockSpec((nb, d), lambda l, j: (0, 0)),
                  pl.BlockSpec((None, d, tn), lambda l, j: (l, 0, j)),
                  pl.BlockSpec((None, 1, tn), lambda l, j: (l, 0, j))],
        out_specs=pl.BlockSpec((None, nb, tn), lambda l, j: (l, 0, j)),
        out_shape=jax.ShapeDtypeStruct((depth, nb, d6), F32),
        compiler_params=_cparams("arbitrary", "arbitrary"), name="ada",
    )(c_all, w_bf, b_ada.reshape(depth, 1, d6))


def _mod_spec(grp, which, tm):
    d = grp["d"]
    if grp["per_row"]:
        return pl.BlockSpec((tm, d), lambda i, *_: (i, which))
    t = grp["t"]
    return pl.BlockSpec((None, None, 1, d), lambda i, *_: ((i * tm) // t, which, 0, 0))


def _proj_kernel(x_ref, sh_ref, sc_ref, w_ref, o_ref, h_scr):
    @pl.when(pl.program_id(1) == 0)
    def _():
        tm = x_ref.shape[0]
        rc = min(tm, 256)

        def chunk(c, carry):
            rows = pl.ds(pl.multiple_of(c * rc, rc), rc)
            vec = lambda ref: ref[...] if ref.shape[0] == 1 else ref[rows, :]
            h = _std(x_ref[rows, :], LN_EPS) * (1.0 + vec(sc_ref)) + vec(sh_ref)
            h_scr[rows, :] = h.astype(BF16)
            return carry

        lax.fori_loop(0, tm // rc, chunk, 0)
    o_ref[...] = jnp.dot(h_scr[...], w_ref[...], preferred_element_type=F32)


def _proj(x, grp, mod, w_bf, tn):
    n, d = x.shape
    p = w_bf.shape[1]
    tm = grp["tm_big"]
    return pl.pallas_call(
        _proj_kernel, grid=(n // tm, p // tn),
        in_specs=[pl.BlockSpec((tm, d), lambda i, j: (i, 0)),
                  _mod_spec(grp, 0, tm), _mod_spec(grp, 1, tm),
                  pl.BlockSpec((d, tn), lambda i, j: (0, j))],
        out_specs=pl.BlockSpec((tm, tn), lambda i, j: (i, j)),
        out_shape=jax.ShapeDtypeStruct((n, p), F32),
        scratch_shapes=[pltpu.VMEM((tm, d), BF16)],
        compiler_params=_cparams("arbitrary", "arbitrary"), name="proj",
    )(x, mod, mod, w_bf)


def _rwkv_prep_kernel(t_seq, tm, da, per_row, pm_ref, pl_ref, prevm_ref, prevl_ref, mum_ref, mul_ref,
                      dbias_ref, abias_ref, wd_ref, wa_ref, wg_ref, kk_ref, ka_ref, seg_ref,
                      r_o, w_o, k_o, v_o, a_o, b_o, og_o, carm, carl):
    i = pl.program_id(0)
    row = lax.broadcasted_iota(jnp.int32, (tm, 1), 0)

    def shifted(p, prev_ref, car):
        rolled = pltpu.roll(p, 1, 0)
        if per_row:
            return jnp.where(lax.rem(row, t_seq) == 0, prev_ref[...], rolled)
        first = jnp.where(lax.rem(i, t_seq // tm) == 0, prev_ref[...], car[...])
        car[...] = p[tm - 1:tm, :]
        return jnp.where(row == 0, first, rolled)

    p = pm_ref[...]
    pm = p + (shifted(p, prevm_ref, carm) - p) * mum_ref[...]
    q = pl_ref[...]
    lo = q + (shifted(q, prevl_ref, carl) - q) * mul_ref[...]
    r = pm[:, :da]
    k = pm[:, da:2 * da]
    v = pm[:, 2 * da:3 * da]
    w_pre = dbias_ref[...] + jnp.dot(jnp.tanh(lo).astype(BF16), wd_ref[...], preferred_element_type=F32)
    decay = jnp.exp(-math.exp(-0.5) * _sigmoid(w_pre))
    a = _sigmoid(abias_ref[...] + jnp.dot(lo.astype(BF16), wa_ref[...], preferred_element_type=F32))
    og = jnp.dot(_sigmoid(lo).astype(BF16), wg_ref[...], preferred_element_type=F32)
    kk = k * kk_ref[...]
    kk = kk * lax.rsqrt(_head_sum(kk * kk, seg_ref[...]) + KK_EPS)
    r_o[...] = r
    w_o[...] = decay
    k_o[...] = k * (1.0 + (a - 1.0) * ka_ref[...])
    v_o[...] = v
    a_o[...] = -kk
    b_o[...] = kk * a
    og_o[...] = og


def _rwkv_prep(proj_main, proj_lora, grp, prev_main, prev_lora, prm, seg):
    n = proj_main.shape[0]
    da, lp = prm["da"], prm["lp"]
    tm = grp["tm_prep"]
    per_row = grp["t"] < tm
    row_spec = lambda w: pl.BlockSpec((tm, w), lambda i: (i, 0))
    vec_spec = lambda w: pl.BlockSpec((1, w), lambda i: (0, 0))
    if per_row:
        prev_specs = [row_spec(3 * da), row_spec(lp)]
    else:
        t = grp["t"]
        prev_specs = [pl.BlockSpec((None, 1, 3 * da), lambda i: ((i * tm) // t, 0, 0)),
                      pl.BlockSpec((None, 1, lp), lambda i: ((i * tm) // t, 0, 0))]
    out = jax.ShapeDtypeStruct((n, da), F32)
    return pl.pallas_call(
        functools.partial(_rwkv_prep_kernel, grp["t"], tm, da, per_row), grid=(n // tm,),
        in_specs=[row_spec(3 * da), row_spec(lp)] + prev_specs + [
            vec_spec(3 * da), vec_spec(lp), vec_spec(da), vec_spec(da),
            pl.BlockSpec((lp, da), lambda i: (0, 0)), pl.BlockSpec((lp, da), lambda i: (0, 0)),
            pl.BlockSpec((lp, da), lambda i: (0, 0)), vec_spec(da), vec_spec(da),
            pl.BlockSpec((LANES, LANES), lambda i: (0, 0))],
        out_specs=[row_spec(da)] * 7, out_shape=[out] * 7,
        scratch_shapes=[pltpu.VMEM((1, 3 * da), F32), pltpu.VMEM((1, lp), F32)],
        compiler_params=_cparams("arbitrary"), name="rwkv_prep",
    )(proj_main, proj_lora, prev_main, prev_lora, prm["mu_main"], prm["mu_lora"], prm["decay_bias"],
      prm["aaa_bias"], prm["wd"], prm["wa"], prm["wg"], prm["k_k"], prm["k_a"], seg)


def _scan_kernel(tt, n_pairs, bb, r_ref, w_ref, k_ref, v_ref, a_ref, b_ref, s0_ref, seg_ref, dm_ref,
                 y_ref, sT_ref, s_scr):
    tc = pl.program_id(1)

    @pl.when(tc == 0)
    def _():
        s_scr[...] = s0_ref[...]

    seg = seg_ref[...]
    diag = dm_ref[...]
    steps = min(tt, SUBLANES)
    chains = [(s, p) for s in range(bb) for p in range(n_pairs)]
    lanes = lambda p: slice(p * LANES, (p + 1) * LANES)

    def block(tb, carry):
        rows = pl.ds(pl.multiple_of(tb * steps, steps), steps)
        r8, w8, k8, v8, a8, b8 = [[ref[s, rows, :] for s in range(bb)]
                                  for ref in (r_ref, w_ref, k_ref, v_ref, a_ref, b_ref)]
        ys = [[] for _ in range(bb)]
        for i in range(steps):
            row = lambda x, c: x[c[0]][i:i + 1, lanes(c[1])]
            st = [s_scr[s, :, lanes(p)] for s, p in chains]
            sa = [_split_dot(st[n] * row(a8, c), seg) for n, c in enumerate(chains)]
            vcol = [_split_dot(diag * row(v8, c), seg) for c in chains]
            st = [st[n] * row(w8, c) + sa[n] * row(b8, c) + vcol[n] * row(k8, c) for n, c in enumerate(chains)]
            for n, (s, p) in enumerate(chains):
                s_scr[s, :, lanes(p)] = st[n]
            yb = [_split_dot(st[n] * row(r8, c), seg) for n, c in enumerate(chains)]
            yv = [jnp.sum(x * diag, axis=0, keepdims=True) for x in yb]
            for s in range(bb):
                part = yv[s * n_pairs:(s + 1) * n_pairs]
                ys[s].append(part[0] if n_pairs == 1 else jnp.concatenate(part, axis=1))
        for s in range(bb):
            y_ref[s, rows, :] = jnp.concatenate(ys[s], axis=0)
        return carry

    lax.fori_loop(0, tt // steps, block, 0)

    @pl.when(tc == pl.num_programs(1) - 1)
    def _():
        sT_ref[...] = s_scr[...]


def _wkv_chunk_kernel(n_sub, n_pairs, c, hd, r_ref, w_ref, k_ref, v_ref, a_ref, b_ref, h0_ref, y_ref, hT_ref, h_scr):
    tc = pl.program_id(1)
    nh = LANES // hd
    rows_st = nh * c

    @pl.when(tc == 0)
    def _():
        h_scr[...] = h0_ref[...]

    lane_head = lax.broadcasted_iota(jnp.int32, (1, LANES), 1) // hd
    ri = lax.broadcasted_iota(jnp.int32, (rows_st, rows_st), 0)
    ci = lax.broadcasted_iota(jnp.int32, (rows_st, rows_st), 1)
    same_head = (ri // c) == (ci // c)
    strict = same_head & (ci < ri)
    incl = same_head & (ci <= ri)
    ti = lax.broadcasted_iota(jnp.int32, (c, c), 0)
    tj = lax.broadcasted_iota(jnp.int32, (c, c), 1)
    tri_incl = jnp.where(tj <= ti, 1.0, 0.0).astype(BF16)
    ones_3c = jnp.ones((3 * c, LANES), BF16)

    def stack(x):
        return jnp.concatenate([jnp.where(lane_head == h, x, 0.0) for h in range(nh)], axis=0).astype(BF16)

    def dot(a, b):
        return jnp.dot(a, b, preferred_element_type=F32)

    def sub_chunk(sc, carry):
        rows = pl.ds(pl.multiple_of(sc * c, c), c)
        pairs = range(n_pairs)
        sls = [slice(p * LANES, (p + 1) * LANES) for p in pairs]
        ld = lambda ref: [ref[rows, sl] for sl in sls]
        r, w, k, v, a, b = ld(r_ref), ld(w_ref), ld(k_ref), ld(v_ref), ld(a_ref), ld(b_ref)
        lw = [jnp.log(x) for x in w]
        cum = [_split_dot(x, tri_incl, left=True) for x in lw]
        p_inv = [jnp.exp(-x) for x in cum]
        a_st = [stack(a[p] * jnp.exp(cum[p] - lw[p])) for p in pairs]
        r_st = [stack(r[p] * jnp.exp(cum[p])) for p in pairs]
        b_st = [stack(b[p] * p_inv[p]) for p in pairs]
        k_st = [stack(k[p] * p_inv[p]) for p in pairs]
        v_st = [stack(x) for x in v]
        h = [h_scr[p] for p in pairs]
        hb = [x.astype(BF16) for x in h]
        cat0 = lambda *xs: jnp.concatenate(xs, axis=0)
        cat1 = lambda *xs: jnp.concatenate(xs, axis=1)
        cross = [_dot_nt(cat0(a_st[p], r_st[p]), cat0(b_st[p], k_st[p])) for p in pairs]
        npow = [jnp.where(strict, x[:rows_st, :rows_st], 0.0).astype(BF16) for x in cross]
        n_ak = [jnp.where(strict, x[:rows_st, rows_st:], 0.0).astype(BF16) for x in cross]
        m_rb = [jnp.where(incl, x[rows_st:, :rows_st], 0.0).astype(BF16) for x in cross]
        m_rk = [jnp.where(incl, x[rows_st:, rows_st:], 0.0).astype(BF16) for x in cross]
        u = [dot(cat1(a_st[p], n_ak[p]), cat0(hb[p], v_st[p])) for p in pairs]
        span = 1
        while span < c:
            span *= 2
            if span < c:
                both = [dot(npow[p], cat1(u[p].astype(BF16), npow[p])) for p in pairs]
                u = [u[p] + both[p][:, :LANES] for p in pairs]
                npow = [x[:, LANES:].astype(BF16) for x in both]
            else:
                u = [u[p] + dot(npow[p], u[p].astype(BF16)) for p in pairs]
        ub = [x.astype(BF16) for x in u]
        for p in pairs:
            y_st = dot(cat1(r_st[p], m_rb[p], m_rk[p]), cat0(hb[p], ub[p], v_st[p]))
            y = y_st[0:c]
            for hh in range(1, nh):
                y = y + y_st[hh * c:(hh + 1) * c]
            y_ref[rows, sls[p]] = y
        for p in pairs:
            l1 = lw[p].astype(BF16)
            r1 = lw[p] - l1.astype(F32)
            l2 = r1.astype(BF16)
            l3 = (r1 - l2.astype(F32)).astype(BF16)
            tot_col = _dot_tn(cat0(l1, l2, l3), ones_3c)
            p_end = jnp.exp(cum[p][c - 1:c, :] - cum[p])
            h_scr[p] = h[p] * jnp.exp(tot_col) + _dot_tn(cat0(stack(b[p] * p_end), stack(k[p] * p_end)),
                                                         cat0(ub[p], v_st[p]))
        return carry

    lax.fori_loop(0, n_sub, sub_chunk, 0)

    @pl.when(tc == pl.num_programs(1) - 1)
    def _():
        hT_ref[...] = h_scr[...]


def _wkv_chunked(rwkv, wkv0, grp, hd):
    b, t = grp["b"], grp["t"]
    da = rwkv[0].shape[1]
    nh = LANES // hd
    n_pairs = da // LANES
    c = LANES // nh
    tt = min(t, 256)
    eye = jnp.eye(nh, dtype=F32)
    h0 = wkv0.transpose(0, 1, 3, 2).reshape(b, n_pairs, nh, hd, 1, hd) * eye[None, None, :, None, :, None]
    h0 = h0.reshape(b, n_pairs, LANES, LANES)
    seq_spec = pl.BlockSpec((None, tt, da), lambda i, j: (i, j, 0))
    st_spec = pl.BlockSpec((None, n_pairs, LANES, LANES), lambda i, j: (i, 0, 0, 0))
    y, h_fin = pl.pallas_call(
        functools.partial(_wkv_chunk_kernel, tt // c, n_pairs, c, hd), grid=(b, t // tt),
        in_specs=[seq_spec] * 6 + [st_spec], out_specs=[seq_spec, st_spec],
        out_shape=[jax.ShapeDtypeStruct((b, t, da), F32), jax.ShapeDtypeStruct((b, n_pairs, LANES, LANES), F32)],
        scratch_shapes=[pltpu.VMEM((n_pairs, LANES, LANES), F32)],
        compiler_params=_cparams("arbitrary", "arbitrary"), name="wkv_chunked",
    )(*[x.reshape(b, t, da) for x in rwkv], h0)
    h6 = h_fin.reshape(b, n_pairs, nh, hd, nh, hd)
    s_fin = jnp.stack([h6[:, :, h, :, h, :] for h in range(nh)], axis=2)
    return y.reshape(b * t, da), s_fin.reshape(b, n_pairs * nh, hd, hd).transpose(0, 1, 3, 2)


def _scan(rwkv, wkv0, grp, seg, diag):
    b, t = grp["b"], grp["t"]
    da = rwkv[0].shape[1]
    hd = wkv0.shape[-1]
    s0 = wkv0.transpose(0, 2, 1, 3).reshape(b, hd, da)
    tt = min(t, 256)
    bb = 4 if (t <= SUBLANES and b % 4 == 0) else 1
    seq_spec = pl.BlockSpec((bb, tt, da), lambda i, j: (i, j, 0))
    st_spec = pl.BlockSpec((bb, hd, da), lambda i, j: (i, 0, 0))
    y, s_fin = pl.pallas_call(
        functools.partial(_scan_kernel, tt, da // LANES, bb), grid=(b // bb, t // tt),
        in_specs=[seq_spec] * 6 + [st_spec, pl.BlockSpec((LANES, LANES), lambda i, j: (0, 0)),
                                   pl.BlockSpec((hd, LANES), lambda i, j: (0, 0))],
        out_specs=[seq_spec, st_spec],
        out_shape=[jax.ShapeDtypeStruct((b, t, da), F32), jax.ShapeDtypeStruct((b, hd, da), F32)],
        scratch_shapes=[pltpu.VMEM((bb, hd, da), F32)],
        compiler_params=_cparams("arbitrary", "arbitrary"), name="wkv_scan",
    )(*[x.reshape(b, t, da) for x in rwkv], s0, seg, diag)
    return y.reshape(b * t, da), s_fin.reshape(b, hd, da // hd, hd).transpose(0, 2, 1, 3)


def _sb_prompt_kernel(tq, hd, scale, bias_ref, q_ref, k_ref, v_ref, tri_ref, o_ref):
    p, i = pl.program_id(1), pl.program_id(2)
    nh = LANES // hd
    lane = lax.broadcasted_iota(jnp.int32, (1, LANES), 1)
    hmasks = [(lane >= hh * hd) & (lane < (hh + 1) * hd) for hh in range(nh)]
    q = q_ref[...] * scale
    qhs = [jnp.where(hm, q, 0.0).astype(BF16) for hm in hmasks]
    biases = [bias_ref[p * nh + hh] for hh in range(nh)]
    tri = tri_ref[...]

    def key_blocks(js, carry, vis):
        acc, runs = carry
        rows = [pl.ds(pl.multiple_of(j * tq, tq), tq) for j in js]
        kbs = [k_ref[r, :].astype(BF16) for r in rows]
        vs = [v_ref[r, :] for r in rows]
        chains = [(g, hh) for g in range(len(js)) for hh in range(nh)]
        zs = [_dot_nt(qhs[hh], kbs[g]) + biases[hh] for g, hh in chains]
        sps = [_softplus(z) for z in zs]
        lks = [-sp if vis is None else jnp.where(vis, -sp, 0.0) for sp in sps]
        cums = [_split_dot(lk, tri) for lk in lks]
        runs = list(runs)
        for n, (g, hh) in enumerate(chains):
            wts = jnp.exp(zs[n] - sps[n] + cums[n] + runs[hh])
            if vis is not None:
                wts = jnp.where(vis, wts, 0.0)
            vh = jnp.where(hmasks[hh], vs[g], 0.0).astype(BF16)
            acc = acc + jnp.dot(wts.astype(BF16), vh, preferred_element_type=F32)
            runs[hh] = runs[hh] + jnp.sum(lks[n], axis=1, keepdims=True)
        return acc, tuple(runs)

    qrow = lax.broadcasted_iota(jnp.int32, (tq, 1), 0)
    kcol = lax.broadcasted_iota(jnp.int32, (1, tq), 1)
    carry = (jnp.zeros((tq, LANES), F32), tuple(jnp.zeros((tq, 1), F32) for _ in range(nh)))
    carry = key_blocks([i], carry, kcol < qrow)
    carry = lax.fori_loop(
        0, i // SB_UNROLL,
        lambda s, c: key_blocks([i - 1 - SB_UNROLL * s - g for g in range(SB_UNROLL)], c, None), carry)
    rem = lax.rem(i, SB_UNROLL)
    carry = lax.fori_loop(0, rem, lambda s, c: key_blocks([rem - 1 - s], c, None), carry)
    o_ref[...] = carry[0]


def _sb_prompt(proj_main, grp, sb_bias, db, hd, q_off):
    b, t = grp["b"], grp["t"]
    n = proj_main.shape[0]
    tq = min(t, 256)
    nq = t // tq
    cb = q_off // LANES
    tri = (jnp.arange(tq)[:, None] > jnp.arange(tq)[None, :]).astype(BF16)
    return pl.pallas_call(
        functools.partial(_sb_prompt_kernel, tq, hd, hd ** -0.5),
        grid=(b, db // LANES, nq),
        in_specs=[pl.BlockSpec(memory_space=pltpu.SMEM),
                  pl.BlockSpec((tq, LANES), lambda bb, p, i: (bb * nq + i, cb + p)),
                  pl.BlockSpec((t, LANES), lambda bb, p, i: (bb, cb + db // LANES + p)),
                  pl.BlockSpec((t, LANES), lambda bb, p, i: (bb, cb + 2 * (db // LANES) + p)),
                  pl.BlockSpec((tq, tq), lambda bb, p, i: (0, 0))],
        out_specs=pl.BlockSpec((tq, LANES), lambda bb, p, i: (bb * nq + i, p)),
        out_shape=jax.ShapeDtypeStruct((n, db), F32),
        compiler_params=_cparams("arbitrary", "arbitrary", "arbitrary"), name="sb_prompt",
    )(sb_bias, proj_main, proj_main, proj_main, tri)


def _sb_sample_kernel(n_pg, n_heads, tp, page, scale, pt_ref, q_ref, knt_ref, vnt_ref, *rest):
    kc_refs, vc_refs = rest[:n_pg], rest[n_pg:2 * n_pg]
    bias_ref, tri_ref, o_ref, qbd_scr, kn_scr, vn_scr, acc_scr, run_scr = rest[2 * n_pg:]
    j = pl.program_id(1)
    rows = n_heads * tp
    tri = tri_ref[...]
    hd = q_ref.shape[2]
    flat = lambda x: x.reshape(n_heads * hd, page).astype(BF16)

    def blocks(kts, vts, vis):
        n = len(kts)
        z_all = jnp.dot(qbd_scr[...], jnp.concatenate([flat(kt) for kt in kts], axis=1),
                        preferred_element_type=F32)
        zs = [z_all[:, g * page:(g + 1) * page] + bias_ref[...] for g in range(n)]
        sps = [_softplus(z) for z in zs]
        lks = [-sp if vis is None else jnp.where(vis, -sp, 0.0) for sp in sps]
        cums = [_split_dot(lk, tri) for lk in lks]
        run = run_scr[...]
        wts = []
        for g in range(n):
            w = jnp.exp(zs[g] - sps[g] + cums[g] + run)
            wts.append((w if vis is None else jnp.where(vis, w, 0.0)).astype(BF16))
            run = run + jnp.sum(lks[g], axis=1, keepdims=True)
        run_scr[...] = run
        acc_scr[...] += _dot_nt(jnp.concatenate(wts, axis=1), jnp.concatenate([flat(vt) for vt in vts], axis=1))

    @pl.when(j == 0)
    def _():
        qbd_scr[...] = jnp.zeros_like(qbd_scr)
        for h in range(n_heads):
            qbd_scr[h * tp:(h + 1) * tp, h * hd:(h + 1) * hd] = (q_ref[h] * scale).astype(BF16)
        kn_scr[...] = jnp.zeros_like(kn_scr)
        vn_scr[...] = jnp.zeros_like(vn_scr)
        kn_scr[:, :, 0:tp] = knt_ref[...]
        vn_scr[:, :, 0:tp] = vnt_ref[...]
        acc_scr[...] = jnp.zeros_like(acc_scr)
        run_scr[...] = jnp.zeros_like(run_scr)
        s_idx = lax.broadcasted_iota(jnp.int32, (rows, page), 1)
        t_idx = lax.rem(lax.broadcasted_iota(jnp.int32, (rows, page), 0), tp)
        blocks([kn_scr[...]], [vn_scr[...]], s_idx < t_idx)

    blocks([ref[...] for ref in kc_refs], [ref[...] for ref in vc_refs], None)

    @pl.when(j == pl.num_programs(1) - 1)
    def _():
        for h in range(n_heads):
            o_ref[h] = acc_scr[h * tp:(h + 1) * tp, h * hd:(h + 1) * hd]


def _sb_sample(proj_main, grp, cache_kt, cache_vt, layer, page_table, sb_bias, db, hd, q_off):
    b, t = grp["b"], grp["t"]
    n_heads = db // hd
    page = cache_kt.shape[-1]
    n_pages = page_table.shape[1]
    tp = SUBLANES
    assert t <= tp
    n_pg = 4 if n_pages % 4 == 0 else (2 if n_pages % 2 == 0 else 1)
    col = lambda i: lax.slice_in_dim(proj_main, q_off + i * db, q_off + (i + 1) * db, axis=1).reshape(b, t, n_heads, hd)
    q = jnp.pad(col(0).transpose(0, 2, 1, 3), ((0, 0), (0, 0), (0, tp - t), (0, 0)))
    new_t = lambda x: jnp.pad(x.transpose(0, 2, 3, 1), ((0, 0), (0, 0), (0, 0), (0, tp - t)))
    bias_b = jnp.broadcast_to(jnp.repeat(sb_bias, tp)[:, None], (n_heads * tp, page))
    tri = (jnp.arange(page)[:, None] > jnp.arange(page)[None, :]).astype(BF16)
    q_spec = pl.BlockSpec((None, n_heads, tp, hd), lambda bb, j, pt: (bb, 0, 0, 0))
    new_spec = pl.BlockSpec((None, n_heads, hd, tp), lambda bb, j, pt: (bb, 0, 0, 0))

    def page_spec(g):
        return pl.BlockSpec((None, None, n_heads, hd, page),
                            lambda bb, j, pt: (layer, pt[bb * n_pages + n_pages - 1 - (j * n_pg + g)], 0, 0, 0))

    const = lambda shape: pl.BlockSpec(shape, lambda bb, j, pt: (0,) * len(shape))
    out = pl.pallas_call(
        functools.partial(_sb_sample_kernel, n_pg, n_heads, tp, page, hd ** -0.5),
        grid_spec=pltpu.PrefetchScalarGridSpec(
            num_scalar_prefetch=1, grid=(b, n_pages // n_pg),
            in_specs=[q_spec, new_spec, new_spec] + [page_spec(g) for g in range(n_pg)] * 2
                     + [const((n_heads * tp, page)), const((page, page))],
            out_specs=q_spec,
            scratch_shapes=[pltpu.VMEM((n_heads * tp, db), BF16),
                            pltpu.VMEM((n_heads, hd, page), F32), pltpu.VMEM((n_heads, hd, page), F32),
                            pltpu.VMEM((n_heads * tp, db), F32), pltpu.VMEM((n_heads * tp, 1), F32)]),
        out_shape=jax.ShapeDtypeStruct((b, n_heads, tp, hd), F32),
        compiler_params=_cparams("arbitrary", "arbitrary"), name="sb_sample",
    )(page_table.reshape(-1), q, new_t(col(1)), new_t(col(2)), *([cache_kt] * n_pg), *([cache_vt] * n_pg),
      bias_b, tri)
    return out[:, :, :t, :].transpose(0, 2, 1, 3).reshape(b * t, db)


def _merge_kernel(y_ref, r_ref, k_ref, v_ref, og_ref, yb_ref, ga_ref, gb_ref, wa_ref, wb_ref,
                  rk_ref, lg_ref, lb_ref, seg_ref, o_ref, ya_scr):
    @pl.when(pl.program_id(1) == 0)
    def _():
        seg = seg_ref[...]
        y = y_ref[...]
        hd_inv = 1.0 / jnp.sum(seg[0:1, :].astype(F32))
        mu = _head_sum(y, seg) * hd_inv
        yc = y - mu
        var = _head_sum(yc * yc, seg) * hd_inv
        yn = yc * lax.rsqrt(var + GN_EPS) * lg_ref[...] + lb_ref[...]
        bonus = _head_sum(r_ref[...] * k_ref[...] * rk_ref[...], seg)
        ya_scr[...] = ((yn + bonus * v_ref[...]) * og_ref[...]).astype(BF16)
    ma = jnp.dot(ya_scr[...], wa_ref[...], preferred_element_type=F32)
    mb = jnp.dot(yb_ref[...].astype(BF16), wb_ref[...], preferred_element_type=F32)
    o_ref[...] = (_sigmoid(ga_ref[...]) * ma + _sigmoid(gb_ref[...]) * mb).astype(BF16)


def _merge(y, rwkv_r, rwkv_k, rwkv_v, og, y_b, proj_main, gate_off, wa_bf, wb_bf, prm, seg, tm):
    n, da = y.shape
    db = y_b.shape[1]
    d = wa_bf.shape[1]
    tn = _tile(d, 512)
    ga_blk = gate_off // tn
    row = lambda w: pl.BlockSpec((tm, w), lambda i, j: (i, 0))
    vec = lambda w: pl.BlockSpec((1, w), lambda i, j: (0, 0))
    return pl.pallas_call(
        _merge_kernel, grid=(n // tm, d // tn),
        in_specs=[row(da)] * 5 + [row(db),
                  pl.BlockSpec((tm, tn), lambda i, j: (i, ga_blk + j)),
                  pl.BlockSpec((tm, tn), lambda i, j: (i, ga_blk + d // tn + j)),
                  pl.BlockSpec((da, tn), lambda i, j: (0, j)), pl.BlockSpec((db, tn), lambda i, j: (0, j)),
                  vec(da), vec(da), vec(da), pl.BlockSpec((LANES, LANES), lambda i, j: (0, 0))],
        out_specs=pl.BlockSpec((tm, tn), lambda i, j: (i, j)),
        out_shape=jax.ShapeDtypeStruct((n, d), BF16),
        scratch_shapes=[pltpu.VMEM((tm, da), BF16)],
        compiler_params=_cparams("arbitrary", "arbitrary"), name="merge",
    )(y, rwkv_r, rwkv_k, rwkv_v, og, y_b, proj_main, proj_main, wa_bf, wb_bf,
      prm["r_k"], prm["lnx_g"], prm["lnx_b"], seg)


def _wo_ln_kernel(alpha, m_ref, w_ref, x_ref, gm_ref, shf_ref, scf_ref, lg_ref, lb_ref, x1_o, h2_o):
    y = jnp.dot(m_ref[...], w_ref[...], preferred_element_type=F32)
    x1 = _std(alpha * x_ref[...] + gm_ref[...] * y, LN_EPS) * lg_ref[...] + lb_ref[...]
    x1_o[...] = x1
    h2_o[...] = (_std(x1, LN_EPS) * (1.0 + scf_ref[...]) + shf_ref[...]).astype(BF16)


def _wo_ln(merged, x, grp, mod, wo_bf, ln_g, ln_b, alpha, tm):
    n, d = x.shape
    row = pl.BlockSpec((tm, d), lambda i: (i, 0))
    vec = pl.BlockSpec((1, d), lambda i: (0, 0))
    return pl.pallas_call(
        functools.partial(_wo_ln_kernel, alpha), grid=(n // tm,),
        in_specs=[row, pl.BlockSpec((d, d), lambda i: (0, 0)), row,
                  _mod_spec(grp, 2, tm), _mod_spec(grp, 3, tm), _mod_spec(grp, 4, tm), vec, vec],
        out_specs=[row, row],
        out_shape=[jax.ShapeDtypeStruct((n, d), F32), jax.ShapeDtypeStruct((n, d), BF16)],
        compiler_params=_cparams("arbitrary"), name="wo_ln",
    )(merged, wo_bf, x, mod, mod, mod, ln_g, ln_b)


def _peer_q_kernel(ng, hk, h_ref, wq_ref, keys_ref, o_ref):
    qp = jnp.dot(h_ref[...], wq_ref[...], preferred_element_type=F32)
    for g in range(ng):
        o_ref[g] = _dot_nt(keys_ref[g], qp[:, g * hk:(g + 1) * hk].astype(BF16))


def _peer_q(h2, wq_bf, keys_bf, tm):
    n, d = h2.shape
    n_grp, nk, hk = keys_bf.shape
    ng = 4 if n_grp % 4 == 0 else 2
    return pl.pallas_call(
        functools.partial(_peer_q_kernel, ng, hk), grid=(n // tm, n_grp // ng),
        in_specs=[pl.BlockSpec((tm, d), lambda i, j: (i, 0)),
                  pl.BlockSpec((d, ng * hk), lambda i, j: (0, j)),
                  pl.BlockSpec((ng, nk, hk), lambda i, j: (j, 0, 0))],
        out_specs=pl.BlockSpec((ng, nk, tm), lambda i, j: (j, 0, i)),
        out_shape=jax.ShapeDtypeStruct((n_grp, nk, n), F32),
        compiler_params=_cparams("arbitrary", "arbitrary"), name="peer_q",
    )(h2, wq_bf, keys_bf)


def _top_rows(s, k, payloads=(), rid=None):
    if rid is None:
        rid = lax.broadcasted_iota(jnp.int32, s.shape, 0).astype(F32)
    vals, idxs, picked = [], [], [[] for _ in payloads]
    for _ in range(k):
        m = jnp.max(s, axis=0, keepdims=True)
        idx = jnp.min(jnp.where(s == m, rid, RID_NONE), axis=0, keepdims=True)
        hit = rid == idx
        vals.append(m)
        idxs.append(idx)
        for out, pay in zip(picked, payloads):
            out.append(jnp.max(jnp.where(hit, pay, -1.0), axis=0, keepdims=True))
        s = jnp.where(hit, -jnp.inf, s)
    return vals, idxs, picked


def _peer_topk_kernel(hp, s_ref, g_o, i1_o, i2_o):
    for hh in range(hp):
        rows = slice(hh * TOPK, (hh + 1) * TOPK)
        g_o[rows, :], i1_o[rows, :], i2_o[rows, :] = _peer_topk_head(s_ref[2 * hh], s_ref[2 * hh + 1])


def _peer_topk_head(s0, s1):
    tl = s0.shape[1]
    v0, i0, _ = _top_rows(s0, TOPK)
    v1, i1, _ = _top_rows(s1, TOPK)
    v0c, i0c = jnp.concatenate(v0, axis=0), jnp.concatenate(i0, axis=0)
    v1c, i1c = jnp.concatenate(v1, axis=0), jnp.concatenate(i1, axis=0)
    sub = lax.broadcasted_iota(jnp.int32, (SUBLANES, tl), 0)
    cand, rid, c1, c2 = [], [], [], []
    for i in range(TOPK // 2):
        nj = TOPK // (i + 1)
        for j0 in range(0, nj, SUBLANES):
            jj = sub + j0
            cand.append(jnp.where(jj < nj, v0[i] + v1c[j0:j0 + SUBLANES], -jnp.inf))
            rid.append((jj + i * TOPK).astype(F32))
            c1.append(jnp.broadcast_to(i0[i], (SUBLANES, tl)))
            c2.append(i1c[j0:j0 + SUBLANES])
    for i0_ in range(TOPK // 2, TOPK, SUBLANES):
        cand.append(v0c[i0_:i0_ + SUBLANES] + v1[0])
        rid.append(((sub + i0_) * TOPK).astype(F32))
        c1.append(i0c[i0_:i0_ + SUBLANES])
        c2.append(jnp.broadcast_to(i1[0], (SUBLANES, tl)))
    cat = lambda xs: jnp.concatenate(xs, axis=0)
    best, _, (e1, e2) = _top_rows(cat(cand), TOPK, (cat(c1), cat(c2)), rid=cat(rid))
    best = jnp.concatenate(best, axis=0)
    e = jnp.exp(best - best[0:1])
    return e / jnp.sum(e, axis=0, keepdims=True), jnp.concatenate(e1, axis=0), jnp.concatenate(e2, axis=0)


def _peer_topk(scores_t):
    n_grp, nk, n = scores_t.shape
    n_heads = n_grp // 2
    tl = _tile(n, LANES)
    hp = 4 if n_heads % 4 == 0 else 1
    out = jax.ShapeDtypeStruct((n_heads * TOPK, n), F32)
    ospec = pl.BlockSpec((hp * TOPK, tl), lambda i, h: (h, i))
    return pl.pallas_call(
        functools.partial(_peer_topk_kernel, hp), grid=(n // tl, n_heads // hp),
        in_specs=[pl.BlockSpec((2 * hp, nk, tl), lambda i, h: (h, 0, i))],
        out_specs=[ospec] * 3, out_shape=[out] * 3,
        compiler_params=_cparams("arbitrary", "arbitrary"), name="peer_topk",
    )(scores_t)


def _peer_gate_kernel(tg, nk, g_ref, i1_ref, i2_ref, o_ref, g_s, i1_s, i2_s, t_scr):
    g_s[...] = g_ref[...].T
    i1_s[...] = i1_ref[...].T
    i2_s[...] = i2_ref[...].T
    m = g_s.shape[1]
    sub = lax.broadcasted_iota(jnp.int32, (nk, m), 0).astype(F32)
    steps = BF16_ROWS

    def body(nb, carry):
        rows = pl.ds(pl.multiple_of(nb * steps, steps), steps)
        g8, a8, b8 = g_s[rows, :], i1_s[rows, :], i2_s[rows, :]
        a_t = [jnp.where(a8[i:i + 1, :] == sub, g8[i:i + 1, :], 0.0).astype(BF16) for i in range(steps)]
        b_t = [jnp.where(b8[i:i + 1, :] == sub, 1.0, 0.0).astype(BF16) for i in range(steps)]
        g_tok = [_dot_nt(a_t[i], b_t[i]) for i in range(steps)]
        for i in range(steps):
            t_scr[pl.ds(i, nk, stride=steps), :] = g_tok[i]
        for c in range(nk):
            o_ref[rows, c * nk:(c + 1) * nk] = t_scr[c * steps:(c + 1) * steps, :].astype(BF16)
        return carry

    lax.fori_loop(0, tg // steps, body, 0)


def _peer_gate(gate_t, i1_t, i2_t, nk):
    m, n = gate_t.shape
    tg = _tile(n, LANES)
    ispec = pl.BlockSpec((m, tg), lambda i: (0, i))
    return pl.pallas_call(
        functools.partial(_peer_gate_kernel, tg, nk), grid=(n // tg,),
        in_specs=[ispec] * 3,
        out_specs=pl.BlockSpec((tg, nk * nk), lambda i: (i, 0)),
        out_shape=jax.ShapeDtypeStruct((n, nk * nk), BF16),
        scratch_shapes=[pltpu.VMEM((tg, m), F32)] * 3 + [pltpu.VMEM((nk * BF16_ROWS, nk), F32)],
        compiler_params=_cparams("arbitrary"), name="peer_gate",
    )(gate_t, i1_t, i2_t)


def _gelu_tanh(x):
    return 0.5 * x * (1.0 + jnp.tanh(math.sqrt(2.0 / math.pi) * (x + 0.044715 * (x * x * x))))


def _peer_dense_kernel(alpha, h_ref, g_ref, ut_ref, v_ref, x1_ref, gf_ref, lg_ref, lb_ref, o_ref):
    c = pl.program_id(1)

    @pl.when(c == 0)
    def _():
        o_ref[...] = jnp.zeros_like(o_ref)

    act = _gelu_tanh(jnp.dot(h_ref[...], ut_ref[...], preferred_element_type=F32))
    p = (act * g_ref[...].astype(F32)).astype(BF16)
    o_ref[...] += jnp.dot(p, v_ref[...], preferred_element_type=F32)

    @pl.when(c == pl.num_programs(1) - 1)
    def _():
        z = alpha * x1_ref[...] + gf_ref[...] * o_ref[...]
        o_ref[...] = _std(z, LN_EPS) * lg_ref[...] + lb_ref[...]


def _peer_dense(h2, gmat, ut_bf, v_bf, x1, grp, mod, ln_g, ln_b, alpha, tm):
    n, d = x1.shape
    e = v_bf.shape[0]
    te = _tile(e, 1024)
    row = pl.BlockSpec((tm, d), lambda i, c: (i, 0))
    vec = pl.BlockSpec((1, d), lambda i, c: (0, 0))
    return pl.pallas_call(
        functools.partial(_peer_dense_kernel, alpha), grid=(n // tm, e // te),
        in_specs=[row, pl.BlockSpec((tm, te), lambda i, c: (i, c)),
                  pl.BlockSpec((d, te), lambda i, c: (0, c)), pl.BlockSpec((te, d), lambda i, c: (c, 0)),
                  row, _mod_spec(grp, 5, tm), vec, vec],
        out_specs=row, out_shape=jax.ShapeDtypeStruct((n, d), F32),
        compiler_params=_cparams("arbitrary", "arbitrary"), name="peer_dense",
    )(h2, gmat, ut_bf, v_bf, x1, mod, ln_g, ln_b)


def _run_trunk(x, grp, mods, shift0, wkv0, cache_k, cache_v, page_table, w, dims):
    d, da, db, hd, lp, depth = dims["d"], dims["da"], dims["db"], dims["hd"], dims["lp"], dims["depth"]
    b, t = grp["b"], grp["t"]
    n = b * t
    tm = grp["tm"]
    alpha = (2 * depth) ** 0.25
    n_raw = 3 * da + dims["n_lora"]
    seg, diag = w["seg"], w["diag"]
    k_rows, v_rows, wkv_rows, shift_rows = [], [], [], []
    for l in range(depth):
        mod = mods[l]
        lw = w["layers"][l]
        proj_main = _proj(x, grp, mod, lw["w_main"], _tile(da, 1024))
        proj_lora = _proj(x, grp, mod, lw["w_lora"], lp)
        if grp["per_row"]:
            prev_main = jnp.repeat(shift0[l][:, :3 * da], t, axis=0)
            prev_lora = jnp.repeat(jnp.pad(shift0[l][:, 3 * da:], ((0, 0), (0, lp - dims["n_lora"]))), t, axis=0)
        else:
            prev_main = shift0[l][:, None, :3 * da]
            prev_lora = jnp.pad(shift0[l][:, None, 3 * da:], ((0, 0), (0, 0), (0, lp - dims["n_lora"])))
        r_, w_, k_, v_, a_, b_, og = _rwkv_prep(proj_main, proj_lora, grp, prev_main, prev_lora, lw, seg)
        if t % (LANES // (LANES // hd)) == 0:
            y, s_fin = _wkv_chunked((r_, w_, k_, v_, a_, b_), wkv0[l], grp, hd)
        else:
            y, s_fin = _scan((r_, w_, k_, v_, a_, b_), wkv0[l], grp, seg, diag)
        if cache_k is None:
            y_b = _sb_prompt(proj_main, grp, lw["sb_bias"], db, hd, 3 * da)
        else:
            y_b = _sb_sample(proj_main, grp, cache_k, cache_v, l, page_table, lw["sb_bias"], db, hd, 3 * da)
        merged = _merge(y, r_, k_, v_, og, y_b, proj_main, 3 * da + 3 * db, lw["w_br_a"], lw["w_br_b"], lw, seg,
                        grp["tm"])
        x1, h2 = _wo_ln(merged, x, grp, mod, lw["wo"], lw["ln1_g"], lw["ln1_b"], alpha, grp["tm_prep"])
        scores_t = _peer_q(h2, lw["wq"], lw["keys"], tm)
        gate_t, i1_t, i2_t = _peer_topk(scores_t)
        nk = lw["keys"].shape[1]
        gmat = _peer_gate(gate_t, i1_t, i2_t, nk)
        x = _peer_dense(h2, gmat, lw["ut"], lw["v"], x1, grp, mod, lw["ln2_g"], lw["ln2_b"], alpha, tm)
        p3 = proj_main.reshape(b, t, -1)
        k_rows.append(p3[:, :, 3 * da + db:3 * da + 2 * db].reshape(b, t, db // hd, hd))
        v_rows.append(p3[:, :, 3 * da + 2 * db:3 * da + 3 * db].reshape(b, t, db // hd, hd))
        wkv_rows.append(s_fin)
        shift_rows.append(jnp.concatenate(
            [p3[:, -1, :3 * da], proj_lora.reshape(b, t, lp)[:, -1, :dims["n_lora"]]], axis=-1))
        assert shift_rows[-1].shape[-1] == n_raw
    return (x.reshape(b, t, d), jnp.stack(k_rows), jnp.stack(v_rows), jnp.stack(wkv_rows), jnp.stack(shift_rows))


def kernel(x_prompt, x_sample, cache_k, cache_v, state_wkv, state_shift, page_table, c_prompt, c_sample,
           w_ada, b_ada, w_in, mu_shift, decay_bias, decay_up, aaa_bias, aaa_up, og_up, k_k, k_a, r_k,
           lnx_g, lnx_b, sb_bias, w_branch_a, w_branch_b, w_o, ln1_g, ln1_b, peer_wq, peer_keys, peer_u, peer_v,
           ln2_g, ln2_b):
    depth, d, _ = w_ada.shape
    hd = cache_k.shape[-1]
    da = decay_up.shape[2]
    db = w_branch_b.shape[1]
    wl, al, gl = decay_up.shape[1], aaa_up.shape[1], og_up.shape[1]
    n_lora = wl + al + gl
    lp = -(-n_lora // LANES) * LANES
    assert da % LANES == 0 and db % LANES == 0 and LANES % hd == 0
    dims = dict(d=d, da=da, db=db, hd=hd, lp=lp, depth=depth, n_lora=n_lora)

    seg = (jnp.arange(LANES)[:, None] // hd == jnp.arange(LANES)[None, :] // hd).astype(BF16)
    diag = (jnp.arange(hd)[:, None] == jnp.arange(LANES)[None, :] % hd).astype(F32)
    layers = []
    for l in range(depth):
        rw = 3 * da + n_lora
        pad_rows = lambda m, off: jnp.pad(m, ((off, lp - off - m.shape[0]), (0, 0))).astype(BF16)
        n_grp = peer_keys.shape[1] * peer_keys.shape[2]
        layers.append(dict(
            da=da, lp=lp,
            w_main=jnp.concatenate([w_in[l][:, :3 * da], w_in[l][:, rw:]], axis=1).astype(BF16),
            w_lora=jnp.pad(w_in[l][:, 3 * da:rw], ((0, 0), (0, lp - n_lora))).astype(BF16),
            mu_main=mu_shift[l][None, :3 * da],
            mu_lora=jnp.pad(mu_shift[l][None, 3 * da:], ((0, 0), (0, lp - n_lora))),
            decay_bias=decay_bias[l][None], aaa_bias=aaa_bias[l][None],
            wd=pad_rows(decay_up[l], 0), wa=pad_rows(aaa_up[l], wl), wg=pad_rows(og_up[l], wl + al),
            w_br_a=w_branch_a[l].astype(BF16), w_br_b=w_branch_b[l].astype(BF16),
            k_k=k_k[l][None], k_a=k_a[l][None], r_k=r_k[l].reshape(1, da),
            lnx_g=lnx_g[l][None], lnx_b=lnx_b[l][None], sb_bias=sb_bias[l],
            wo=w_o[l].astype(BF16), ln1_g=ln1_g[l][None], ln1_b=ln1_b[l][None],
            wq=peer_wq[l].astype(BF16), keys=peer_keys[l].reshape(n_grp, *peer_keys.shape[3:]).astype(BF16),
            ut=peer_u[l].T.astype(BF16), v=peer_v[l].astype(BF16),
            ln2_g=ln2_g[l][None], ln2_b=ln2_b[l][None]))
    w = dict(seg=seg, diag=diag, layers=layers)

    bp, tp_, _ = x_prompt.shape
    bs, ts, _ = x_sample.shape
    mod_all = _ada(jnp.concatenate([c_prompt, c_sample], axis=0), w_ada.astype(BF16), b_ada)

    def group(b, t):
        n = b * t
        tm = _tile(n, 512) if t < 512 else _tile(t, 512)
        tm_prep = _tile(n, 256) if t < 256 else _tile(t, 256)
        tm_big = _tile(n, 1024) if t < tm else _tile(t, 1024)
        return dict(b=b, t=t, d=d, tm=tm, tm_prep=tm_prep, tm_big=tm_big, per_row=t < tm)

    gp, gs = group(bp, tp_), group(bs, ts)
    assert gp["per_row"] == (gp["t"] < gp["tm_prep"]) and gs["per_row"] == (gs["t"] < gs["tm_prep"])

    def mods_for(grp, lo, hi):
        out = []
        for l in range(depth):
            m = mod_all[l, lo:hi]
            out.append(jnp.repeat(m, grp["t"], axis=0) if grp["per_row"] else m.reshape(hi - lo, 6, 1, d))
        return out

    shift0_p = jnp.zeros((depth, bp, 3 * da + n_lora), F32)
    wkv0_p = jnp.zeros((depth, bp, da // hd, hd, hd), F32)
    y_p, k_p, v_p, wkv_p, shift_p = _run_trunk(
        x_prompt.reshape(bp * tp_, d), gp, mods_for(gp, 0, bp), shift0_p, wkv0_p, None, None, None, w, dims)
    y_s, k_s, v_s, wkv_s, shift_s = _run_trunk(
        x_sample.reshape(bs * ts, d), gs, mods_for(gs, bp, bp + bs), state_shift, state_wkv,
        cache_k.transpose(0, 1, 3, 4, 2), cache_v.transpose(0, 1, 3, 4, 2), page_table, w, dims)
    return (y_p, y_s, k_p, v_p, wkv_p, shift_p, k_s, v_s, wkv_s, shift_s)
```

```python
import functools
import math

import jax
import jax.numpy as jnp
from jax import lax
from jax.experimental import pallas as pl
from jax.experimental.pallas import tpu as pltpu

F32 = jnp.float32
BF16 = jnp.bfloat16
LN_EPS = 1e-5
GN_EPS = 64e-5
KK_EPS = 1e-12
TOPK = 16
LANES = 128
SUBLANES = 8
BF16_ROWS = 16
RID_NONE = 1e9
SB_UNROLL = 2
VMEM_LIMIT = 56 * 1024 * 1024


def _cparams(*sem):
    return pltpu.CompilerParams(dimension_semantics=sem, vmem_limit_bytes=VMEM_LIMIT)


def _tile(n, pref):
    if n <= pref:
        return n
    t = (pref // LANES) * LANES
    while t > LANES and n % t:
        t -= LANES
    assert n % t == 0, (n, pref)
    return t


def _std(x, eps):
    mu = jnp.mean(x, axis=-1, keepdims=True)
    xc = x - mu
    var = jnp.mean(xc * xc, axis=-1, keepdims=True)
    return xc * lax.rsqrt(var + eps)


def _sigmoid(x):
    return 1.0 / (1.0 + jnp.exp(-x))


def _softplus(x):
    return jnp.maximum(x, 0.0) + jnp.log(1.0 + jnp.exp(-jnp.abs(x)))


def _split_dot(x, m, *, left=False):
    hi = x.astype(BF16)
    lo = (x - hi.astype(F32)).astype(BF16)
    if left:
        return (jnp.dot(m, hi, preferred_element_type=F32) + jnp.dot(m, lo, preferred_element_type=F32))
    return (jnp.dot(hi, m, preferred_element_type=F32) + jnp.dot(lo, m, preferred_element_type=F32))


def _head_sum(x, seg):
    outs = [_split_dot(x[:, c * LANES:(c + 1) * LANES], seg) for c in range(x.shape[1] // LANES)]
    return outs[0] if len(outs) == 1 else jnp.concatenate(outs, axis=1)


def _dot_nt(a, b):
    return lax.dot_general(a, b, (((1,), (1,)), ((), ())), preferred_element_type=F32)


def _dot_tn(a, b):
    return lax.dot_general(a, b, (((0,), (0,)), ((), ())), preferred_element_type=F32)


def _ada_kernel(c_ref, w_ref, b_ref, o_ref):
    c = c_ref[...]
    s = (c * _sigmoid(c)).astype(BF16)
    o_ref[...] = jnp.dot(s, w_ref[...].astype(BF16), preferred_element_type=F32) + b_ref[...]


def _ada(c_all, w_ada, b_ada):
    depth, d, d6 = w_ada.shape
    nb = c_all.shape[0]
    tn = _tile(d6, 1024)
    return pl.pallas_call(
        _ada_kernel, grid=(depth, d6 // tn),
        in_specs=[pl.BlockSpec((nb, d), lambda l, j: (0, 0)),
                  pl.BlockSpec((None, d, tn), lambda l, j: (l, 0, j)),
                  pl.BlockSpec((None, 1, tn), lambda l, j: (l, 0, j))],
        out_specs=pl.BlockSpec((None, nb, tn), lambda l, j: (l, 0, j)),
        out_shape=jax.ShapeDtypeStruct((depth, nb, d6), F32),
        compiler_params=_cparams("arbitrary", "arbitrary"), name="ada",
    )(c_all, w_ada, b_ada.reshape(depth, 1, d6))


def _mod_spec(grp, which, tm):
    d = grp["d"]
    if grp["per_row"]:
        return pl.BlockSpec((tm, d), lambda i, *_: (i, which))
    t = grp["t"]
    return pl.BlockSpec((None, None, 1, d), lambda i, *_: ((i * tm) // t, which, 0, 0))


def _proj_kernel(x_ref, sh_ref, sc_ref, w_ref, o_ref, h_scr):
    @pl.when(pl.program_id(1) == 0)
    def _():
        tm = x_ref.shape[0]
        rc = min(tm, 256)

        def chunk(c, carry):
            rows = pl.ds(pl.multiple_of(c * rc, rc), rc)
            vec = lambda ref: ref[...] if ref.shape[0] == 1 else ref[rows, :]
            h = _std(x_ref[rows, :], LN_EPS) * (1.0 + vec(sc_ref)) + vec(sh_ref)
            h_scr[rows, :] = h.astype(BF16)
            return carry

        lax.fori_loop(0, tm // rc, chunk, 0)
    o_ref[...] = jnp.dot(h_scr[...], w_ref[...], preferred_element_type=F32)


def _proj(x, grp, mod, w_bf, tn):
    n, d = x.shape
    p = w_bf.shape[1]
    tm = grp["tm_big"]
    return pl.pallas_call(
        _proj_kernel, grid=(n // tm, p // tn),
        in_specs=[pl.BlockSpec((tm, d), lambda i, j: (i, 0)),
                  _mod_spec(grp, 0, tm), _mod_spec(grp, 1, tm),
                  pl.BlockSpec((d, tn), lambda i, j: (0, j))],
        out_specs=pl.BlockSpec((tm, tn), lambda i, j: (i, j)),
        out_shape=jax.ShapeDtypeStruct((n, p), F32),
        scratch_shapes=[pltpu.VMEM((tm, d), BF16)],
        compiler_params=_cparams("arbitrary", "arbitrary"), name="proj",
    )(x, mod, mod, w_bf)


def _rwkv_prep_kernel(t_seq, tm, da, per_row, pm_ref, pl_ref, prevm_ref, prevl_ref, mum_ref, mul_ref,
                      dbias_ref, abias_ref, wd_ref, wa_ref, wg_ref, kk_ref, ka_ref, seg_ref,
                      r_o, w_o, k_o, v_o, a_o, b_o, og_o, carm, carl):
    i = pl.program_id(0)
    row = lax.broadcasted_iota(jnp.int32, (tm, 1), 0)

    def shifted(p, prev_ref, car):
        rolled = pltpu.roll(p, 1, 0)
        if per_row:
            return jnp.where(lax.rem(row, t_seq) == 0, prev_ref[...], rolled)
        first = jnp.where(lax.rem(i, t_seq // tm) == 0, prev_ref[...], car[...])
        car[...] = p[tm - 1:tm, :]
        return jnp.where(row == 0, first, rolled)

    p = pm_ref[...]
    pm = p + (shifted(p, prevm_ref, carm) - p) * mum_ref[...]
    q = pl_ref[...]
    lo = q + (shifted(q, prevl_ref, carl) - q) * mul_ref[...]
    r = pm[:, :da]
    k = pm[:, da:2 * da]
    v = pm[:, 2 * da:3 * da]
    w_pre = dbias_ref[...] + jnp.dot(jnp.tanh(lo).astype(BF16), wd_ref[...], preferred_element_type=F32)
    decay = jnp.exp(-math.exp(-0.5) * _sigmoid(w_pre))
    a = _sigmoid(abias_ref[...] + jnp.dot(lo.astype(BF16), wa_ref[...], preferred_element_type=F32))
    og = jnp.dot(_sigmoid(lo).astype(BF16), wg_ref[...], preferred_element_type=F32)
    kk = k * kk_ref[...]
    kk = kk * lax.rsqrt(_head_sum(kk * kk, seg_ref[...]) + KK_EPS)
    r_o[...] = r
    w_o[...] = decay
    k_o[...] = k * (1.0 + (a - 1.0) * ka_ref[...])
    v_o[...] = v
    a_o[...] = -kk
    b_o[...] = kk * a
    og_o[...] = og


def _rwkv_prep(proj_main, proj_lora, grp, prev_main, prev_lora, prm, seg):
    n = proj_main.shape[0]
    da, lp = prm["da"], prm["lp"]
    tm = grp["tm_prep"]
    per_row = grp["t"] < tm
    row_spec = lambda w: pl.BlockSpec((tm, w), lambda i: (i, 0))
    vec_spec = lambda w: pl.BlockSpec((1, w), lambda i: (0, 0))
    if per_row:
        prev_specs = [row_spec(3 * da), row_spec(lp)]
    else:
        t = grp["t"]
        prev_specs = [pl.BlockSpec((None, 1, 3 * da), lambda i: ((i * tm) // t, 0, 0)),
                      pl.BlockSpec((None, 1, lp), lambda i: ((i * tm) // t, 0, 0))]
    out = jax.ShapeDtypeStruct((n, da), F32)
    return pl.pallas_call(
        functools.partial(_rwkv_prep_kernel, grp["t"], tm, da, per_row), grid=(n // tm,),
        in_specs=[row_spec(3 * da), row_spec(lp)] + prev_specs + [
            vec_spec(3 * da), vec_spec(lp), vec_spec(da), vec_spec(da),
            pl.BlockSpec((lp, da), lambda i: (0, 0)), pl.BlockSpec((lp, da), lambda i: (0, 0)),
            pl.BlockSpec((lp, da), lambda i: (0, 0)), vec_spec(da), vec_spec(da),
            pl.BlockSpec((LANES, LANES), lambda i: (0, 0))],
        out_specs=[row_spec(da)] * 7, out_shape=[out] * 7,
        scratch_shapes=[pltpu.VMEM((1, 3 * da), F32), pltpu.VMEM((1, lp), F32)],
        compiler_params=_cparams("arbitrary"), name="rwkv_prep",
    )(proj_main, proj_lora, prev_main, prev_lora, prm["mu_main"], prm["mu_lora"], prm["decay_bias"],
      prm["aaa_bias"], prm["wd"], prm["wa"], prm["wg"], prm["k_k"], prm["k_a"], seg)


def _scan_kernel(tt, n_pairs, bb, r_ref, w_ref, k_ref, v_ref, a_ref, b_ref, s0_ref, seg_ref, dm_ref,
                 y_ref, sT_ref, s_scr):
    tc = pl.program_id(1)

    @pl.when(tc == 0)
    def _():
        s_scr[...] = s0_ref[...]

    seg = seg_ref[...]
    diag = dm_ref[...]
    steps = min(tt, SUBLANES)
    chains = [(s, p) for s in range(bb) for p in range(n_pairs)]
    lanes = lambda p: slice(p * LANES, (p + 1) * LANES)

    def block(tb, carry):
        rows = pl.ds(pl.multiple_of(tb * steps, steps), steps)
        r8, w8, k8, v8, a8, b8 = [[ref[s, rows, :] for s in range(bb)]
                                  for ref in (r_ref, w_ref, k_ref, v_ref, a_ref, b_ref)]
        ys = [[] for _ in range(bb)]
        for i in range(steps):
            row = lambda x, c: x[c[0]][i:i + 1, lanes(c[1])]
            st = [s_scr[s, :, lanes(p)] for s, p in chains]
            sa = [_split_dot(st[n] * row(a8, c), seg) for n, c in enumerate(chains)]
            vcol = [_split_dot(diag * row(v8, c), seg) for c in chains]
            st = [st[n] * row(w8, c) + sa[n] * row(b8, c) + vcol[n] * row(k8, c) for n, c in enumerate(chains)]
            for n, (s, p) in enumerate(chains):
                s_scr[s, :, lanes(p)] = st[n]
            yb = [_split_dot(st[n] * row(r8, c), seg) for n, c in enumerate(chains)]
            yv = [jnp.sum(x * diag, axis=0, keepdims=True) for x in yb]
            for s in range(bb):
                part = yv[s * n_pairs:(s + 1) * n_pairs]
                ys[s].append(part[0] if n_pairs == 1 else jnp.concatenate(part, axis=1))
        for s in range(bb):
            y_ref[s, rows, :] = jnp.concatenate(ys[s], axis=0)
        return carry

    lax.fori_loop(0, tt // steps, block, 0)

    @pl.when(tc == pl.num_programs(1) - 1)
    def _():
        sT_ref[...] = s_scr[...]


def _wkv_chunk_kernel(n_sub, n_pairs, c, hd, r_ref, w_ref, k_ref, v_ref, a_ref, b_ref, h0_ref, y_ref, hT_ref, h_scr):
    tc = pl.program_id(1)
    nh = LANES // hd
    rows_st = nh * c

    @pl.when(tc == 0)
    def _():
        h_scr[...] = h0_ref[...]

    lane_head = lax.broadcasted_iota(jnp.int32, (1, LANES), 1) // hd
    ri = lax.broadcasted_iota(jnp.int32, (rows_st, rows_st), 0)
    ci = lax.broadcasted_iota(jnp.int32, (rows_st, rows_st), 1)
    same_head = (ri // c) == (ci // c)
    strict = same_head & (ci < ri)
    incl = same_head & (ci <= ri)
    ti = lax.broadcasted_iota(jnp.int32, (c, c), 0)
    tj = lax.broadcasted_iota(jnp.int32, (c, c), 1)
    tri_incl = jnp.where(tj <= ti, 1.0, 0.0).astype(BF16)
    ones_3c = jnp.ones((3 * c, LANES), BF16)

    def stack(x):
        return jnp.concatenate([jnp.where(lane_head == h, x, 0.0) for h in range(nh)], axis=0).astype(BF16)

    def dot(a, b):
        return jnp.dot(a, b, preferred_element_type=F32)

    def sub_chunk(sc, carry):
        rows = pl.ds(pl.multiple_of(sc * c, c), c)
        pairs = range(n_pairs)
        sls = [slice(p * LANES, (p + 1) * LANES) for p in pairs]
        ld = lambda ref: [ref[rows, sl] for sl in sls]
        r, w, k, v, a, b = ld(r_ref), ld(w_ref), ld(k_ref), ld(v_ref), ld(a_ref), ld(b_ref)
        lw = [jnp.log(x) for x in w]
        cum = [_split_dot(x, tri_incl, left=True) for x in lw]
        p_inv = [jnp.exp(-x) for x in cum]
        a_st = [stack(a[p] * jnp.exp(cum[p] - lw[p])) for p in pairs]
        r_st = [stack(r[p] * jnp.exp(cum[p])) for p in pairs]
        b_st = [stack(b[p] * p_inv[p]) for p in pairs]
        k_st = [stack(k[p] * p_inv[p]) for p in pairs]
        v_st = [stack(x) for x in v]
        h = [h_scr[p] for p in pairs]
        hb = [x.astype(BF16) for x in h]
        cat0 = lambda *xs: jnp.concatenate(xs, axis=0)
        cat1 = lambda *xs: jnp.concatenate(xs, axis=1)
        cross = [_dot_nt(cat0(a_st[p], r_st[p]), cat0(b_st[p], k_st[p])) for p in pairs]
        npow = [jnp.where(strict, x[:rows_st, :rows_st], 0.0).astype(BF16) for x in cross]
        n_ak = [jnp.where(strict, x[:rows_st, rows_st:], 0.0).astype(BF16) for x in cross]
        m_rb = [jnp.where(incl, x[rows_st:, :rows_st], 0.0).astype(BF16) for x in cross]
        m_rk = [jnp.where(incl, x[rows_st:, rows_st:], 0.0).astype(BF16) for x in cross]
        u = [dot(cat1(a_st[p], n_ak[p]), cat0(hb[p], v_st[p])) for p in pairs]
        span = 1
        while span < c:
            span *= 2
            if span < c:
                both = [dot(npow[p], cat1(u[p].astype(BF16), npow[p])) for p in pairs]
                u = [u[p] + both[p][:, :LANES] for p in pairs]
                npow = [x[:, LANES:].astype(BF16) for x in both]
            else:
                u = [u[p] + dot(npow[p], u[p].astype(BF16)) for p in pairs]
        ub = [x.astype(BF16) for x in u]
        for p in pairs:
            y_st = dot(cat1(r_st[p], m_rb[p], m_rk[p]), cat0(hb[p], ub[p], v_st[p]))
            y = y_st[0:c]
            for hh in range(1, nh):
                y = y + y_st[hh * c:(hh + 1) * c]
            y_ref[rows, sls[p]] = y
        for p in pairs:
            l1 = lw[p].astype(BF16)
            r1 = lw[p] - l1.astype(F32)
            l2 = r1.astype(BF16)
            l3 = (r1 - l2.astype(F32)).astype(BF16)
            tot_col = _dot_tn(cat0(l1, l2, l3), ones_3c)
            p_end = jnp.exp(cum[p][c - 1:c, :] - cum[p])
            h_scr[p] = h[p] * jnp.exp(tot_col) + _dot_tn(cat0(stack(b[p] * p_end), stack(k[p] * p_end)),
                                                         cat0(ub[p], v_st[p]))
        return carry

    lax.fori_loop(0, n_sub, sub_chunk, 0)

    @pl.when(tc == pl.num_programs(1) - 1)
    def _():
        hT_ref[...] = h_scr[...]


def _wkv_chunked(rwkv, wkv0, grp, hd):
    b, t = grp["b"], grp["t"]
    da = rwkv[0].shape[1]
    nh = LANES // hd
    n_pairs = da // LANES
    c = LANES // nh
    tt = min(t, 256)
    eye = jnp.eye(nh, dtype=F32)
    h0 = wkv0.transpose(0, 1, 3, 2).reshape(b, n_pairs, nh, hd, 1, hd) * eye[None, None, :, None, :, None]
    h0 = h0.reshape(b, n_pairs, LANES, LANES)
    seq_spec = pl.BlockSpec((None, tt, da), lambda i, j: (i, j, 0))
    st_spec = pl.BlockSpec((None, n_pairs, LANES, LANES), lambda i, j: (i, 0, 0, 0))
    y, h_fin = pl.pallas_call(
        functools.partial(_wkv_chunk_kernel, tt // c, n_pairs, c, hd), grid=(b, t // tt),
        in_specs=[seq_spec] * 6 + [st_spec], out_specs=[seq_spec, st_spec],
        out_shape=[jax.ShapeDtypeStruct((b, t, da), F32), jax.ShapeDtypeStruct((b, n_pairs, LANES, LANES), F32)],
        scratch_shapes=[pltpu.VMEM((n_pairs, LANES, LANES), F32)],
        compiler_params=_cparams("arbitrary", "arbitrary"), name="wkv_chunked",
    )(*[x.reshape(b, t, da) for x in rwkv], h0)
    h6 = h_fin.reshape(b, n_pairs, nh, hd, nh, hd)
    s_fin = jnp.stack([h6[:, :, h, :, h, :] for h in range(nh)], axis=2)
    return y.reshape(b * t, da), s_fin.reshape(b, n_pairs * nh, hd, hd).transpose(0, 1, 3, 2)


def _scan(rwkv, wkv0, grp, seg, diag):
    b, t = grp["b"], grp["t"]
    da = rwkv[0].shape[1]
    hd = wkv0.shape[-1]
    s0 = wkv0.transpose(0, 2, 1, 3).reshape(b, hd, da)
    tt = min(t, 256)
    bb = 4 if (t <= SUBLANES and b % 4 == 0) else 1
    seq_spec = pl.BlockSpec((bb, tt, da), lambda i, j: (i, j, 0))
    st_spec = pl.BlockSpec((bb, hd, da), lambda i, j: (i, 0, 0))
    y, s_fin = pl.pallas_call(
        functools.partial(_scan_kernel, tt, da // LANES, bb), grid=(b // bb, t // tt),
        in_specs=[seq_spec] * 6 + [st_spec, pl.BlockSpec((LANES, LANES), lambda i, j: (0, 0)),
                                   pl.BlockSpec((hd, LANES), lambda i, j: (0, 0))],
        out_specs=[seq_spec, st_spec],
        out_shape=[jax.ShapeDtypeStruct((b, t, da), F32), jax.ShapeDtypeStruct((b, hd, da), F32)],
        scratch_shapes=[pltpu.VMEM((bb, hd, da), F32)],
        compiler_params=_cparams("arbitrary", "arbitrary"), name="wkv_scan",
    )(*[x.reshape(b, t, da) for x in rwkv], s0, seg, diag)
    return y.reshape(b * t, da), s_fin.reshape(b, hd, da // hd, hd).transpose(0, 2, 1, 3)


def _sb_prompt_kernel(tq, hd, scale, bias_ref, q_ref, k_ref, v_ref, tri_ref, ones_ref, o_ref):
    p, i = pl.program_id(1), pl.program_id(2)
    nh = LANES // hd
    lane = lax.broadcasted_iota(jnp.int32, (1, LANES), 1)
    hmasks = [(lane >= hh * hd) & (lane < (hh + 1) * hd) for hh in range(nh)]
    q = q_ref[...] * scale
    qhs = [jnp.where(hm, q, 0.0).astype(BF16) for hm in hmasks]
    biases = [bias_ref[p * nh + hh] for hh in range(nh)]
    tri = tri_ref[...]

    def key_blocks(js, carry, vis):
        acc, runs = carry
        rows = [pl.ds(pl.multiple_of(j * tq, tq), tq) for j in js]
        kbs = [k_ref[r, :].astype(BF16) for r in rows]
        vs = [v_ref[r, :] for r in rows]
        chains = [(g, hh) for g in range(len(js)) for hh in range(nh)]
        zs = [_dot_nt(qhs[hh], kbs[g]) + biases[hh] for g, hh in chains]
        sps = [_softplus(z) for z in zs]
        nlk = [(sp if vis is None else jnp.where(vis, sp, 0.0)).astype(BF16) for sp in sps]
        cums = [jnp.dot(x, tri, preferred_element_type=F32) for x in nlk]
        tots = [jnp.dot(x, ones_ref[...], preferred_element_type=F32)[:, 0:1] for x in nlk]
        runs = list(runs)
        wts = []
        for n, (g, hh) in enumerate(chains):
            w = jnp.exp(zs[n] - sps[n] - cums[n] - runs[hh])
            wts.append((w if vis is None else jnp.where(vis, w, 0.0)).astype(BF16))
            runs[hh] = runs[hh] + tots[n]
        vhs = [jnp.where(hmasks[hh], vs[g], 0.0).astype(BF16) for g, hh in chains]
        acc = acc + jnp.dot(jnp.concatenate(wts, axis=1), jnp.concatenate(vhs, axis=0),
                            preferred_element_type=F32)
        return acc, tuple(runs)

    qrow = lax.broadcasted_iota(jnp.int32, (tq, 1), 0)
    kcol = lax.broadcasted_iota(jnp.int32, (1, tq), 1)
    carry = (jnp.zeros((tq, LANES), F32), tuple(jnp.zeros((tq, 1), F32) for _ in range(nh)))
    carry = key_blocks([i], carry, kcol < qrow)
    carry = lax.fori_loop(
        0, i // SB_UNROLL,
        lambda s, c: key_blocks([i - 1 - SB_UNROLL * s - g for g in range(SB_UNROLL)], c, None), carry)
    rem = lax.rem(i, SB_UNROLL)
    carry = lax.fori_loop(0, rem, lambda s, c: key_blocks([rem - 1 - s], c, None), carry)
    o_ref[...] = carry[0]


def _sb_prompt(proj_main, grp, sb_bias, db, hd, q_off):
    b, t = grp["b"], grp["t"]
    n = proj_main.shape[0]
    tq = min(t, 256)
    nq = t // tq
    cb = q_off // LANES
    tri = (jnp.arange(tq)[:, None] > jnp.arange(tq)[None, :]).astype(BF16)
    return pl.pallas_call(
        functools.partial(_sb_prompt_kernel, tq, hd, hd ** -0.5),
        grid=(b, db // LANES, nq),
        in_specs=[pl.BlockSpec(memory_space=pltpu.SMEM),
                  pl.BlockSpec((tq, LANES), lambda bb, p, i: (bb * nq + i, cb + p)),
                  pl.BlockSpec((t, LANES), lambda bb, p, i: (bb, cb + db // LANES + p)),
                  pl.BlockSpec((t, LANES), lambda bb, p, i: (bb, cb + 2 * (db // LANES) + p)),
                  pl.BlockSpec((tq, tq), lambda bb, p, i: (0, 0)),
                  pl.BlockSpec((tq, LANES), lambda bb, p, i: (0, 0))],
        out_specs=pl.BlockSpec((tq, LANES), lambda bb, p, i: (bb * nq + i, p)),
        out_shape=jax.ShapeDtypeStruct((n, db), F32),
        compiler_params=_cparams("arbitrary", "arbitrary", "arbitrary"), name="sb_prompt",
    )(sb_bias, proj_main, proj_main, proj_main, tri, jnp.ones((tq, LANES), BF16))


def _sb_sample_kernel(n_pg, n_heads, tp, page, scale, pt_ref, q_ref, knt_ref, vnt_ref, *rest):
    kc_refs, vc_refs = rest[:n_pg], rest[n_pg:2 * n_pg]
    bias_ref, tri_ref, o_ref, qbd_scr, kn_scr, vn_scr, acc_scr, run_scr = rest[2 * n_pg:]
    j = pl.program_id(1)
    rows = n_heads * tp
    tri = tri_ref[...]
    hd = q_ref.shape[2]
    flat = lambda x: x.reshape(n_heads * hd, page).astype(BF16)

    def blocks(kts, vts, vis):
        n = len(kts)
        z_all = jnp.dot(qbd_scr[...], jnp.concatenate([flat(kt) for kt in kts], axis=1),
                        preferred_element_type=F32)
        zs = [z_all[:, g * page:(g + 1) * page] + bias_ref[...] for g in range(n)]
        sps = [_softplus(z) for z in zs]
        nlk = [sp if vis is None else jnp.where(vis, sp, 0.0) for sp in sps]
        cums = [jnp.dot(x.astype(BF16), tri, preferred_element_type=F32) for x in nlk]
        run = run_scr[...]
        wts = []
        for g in range(n):
            w = jnp.exp(zs[g] - sps[g] - cums[g] - run)
            wts.append((w if vis is None else jnp.where(vis, w, 0.0)).astype(BF16))
            run = run + jnp.sum(nlk[g], axis=1, keepdims=True)
        run_scr[...] = run
        acc_scr[...] += _dot_nt(jnp.concatenate(wts, axis=1), jnp.concatenate([flat(vt) for vt in vts], axis=1))

    @pl.when(j == 0)
    def _():
        qbd_scr[...] = jnp.zeros_like(qbd_scr)
        for h in range(n_heads):
            qbd_scr[h * tp:(h + 1) * tp, h * hd:(h + 1) * hd] = (q_ref[h] * scale).astype(BF16)
        kn_scr[...] = jnp.zeros_like(kn_scr)
        vn_scr[...] = jnp.zeros_like(vn_scr)
        kn_scr[:, :, 0:tp] = knt_ref[...]
        vn_scr[:, :, 0:tp] = vnt_ref[...]
        acc_scr[...] = jnp.zeros_like(acc_scr)
        run_scr[...] = jnp.zeros_like(run_scr)
        s_idx = lax.broadcasted_iota(jnp.int32, (rows, page), 1)
        t_idx = lax.rem(lax.broadcasted_iota(jnp.int32, (rows, page), 0), tp)
        blocks([kn_scr[...]], [vn_scr[...]], s_idx < t_idx)

    blocks([ref[...] for ref in kc_refs], [ref[...] for ref in vc_refs], None)

    @pl.when(j == pl.num_programs(1) - 1)
    def _():
        for h in range(n_heads):
            o_ref[h] = acc_scr[h * tp:(h + 1) * tp, h * hd:(h + 1) * hd]


def _sb_sample(proj_main, grp, cache_kt, cache_vt, layer, page_table, sb_bias, db, hd, q_off):
    b, t = grp["b"], grp["t"]
    n_heads = db // hd
    page = cache_kt.shape[-1]
    n_pages = page_table.shape[1]
    tp = SUBLANES
    assert t <= tp
    n_pg = next(g for g in (8, 4, 2, 1) if n_pages % g == 0)
    col = lambda i: lax.slice_in_dim(proj_main, q_off + i * db, q_off + (i + 1) * db, axis=1).reshape(b, t, n_heads, hd)
    q = jnp.pad(col(0).transpose(0, 2, 1, 3), ((0, 0), (0, 0), (0, tp - t), (0, 0)))
    new_t = lambda x: jnp.pad(x.transpose(0, 2, 3, 1), ((0, 0), (0, 0), (0, 0), (0, tp - t)))
    bias_b = jnp.broadcast_to(jnp.repeat(sb_bias, tp)[:, None], (n_heads * tp, page))
    tri = (jnp.arange(page)[:, None] > jnp.arange(page)[None, :]).astype(BF16)
    q_spec = pl.BlockSpec((None, n_heads, tp, hd), lambda bb, j, pt: (bb, 0, 0, 0))
    new_spec = pl.BlockSpec((None, n_heads, hd, tp), lambda bb, j, pt: (bb, 0, 0, 0))

    def page_spec(g):
        return pl.BlockSpec((None, None, n_heads, hd, page),
                            lambda bb, j, pt: (layer, pt[bb * n_pages + n_pages - 1 - (j * n_pg + g)], 0, 0, 0))

    const = lambda shape: pl.BlockSpec(shape, lambda bb, j, pt: (0,) * len(shape))
    out = pl.pallas_call(
        functools.partial(_sb_sample_kernel, n_pg, n_heads, tp, page, hd ** -0.5),
        grid_spec=pltpu.PrefetchScalarGridSpec(
            num_scalar_prefetch=1, grid=(b, n_pages // n_pg),
            in_specs=[q_spec, new_spec, new_spec] + [page_spec(g) for g in range(n_pg)] * 2
                     + [const((n_heads * tp, page)), const((page, page))],
            out_specs=q_spec,
            scratch_shapes=[pltpu.VMEM((n_heads * tp, db), BF16),
                            pltpu.VMEM((n_heads, hd, page), F32), pltpu.VMEM((n_heads, hd, page), F32),
                            pltpu.VMEM((n_heads * tp, db), F32), pltpu.VMEM((n_heads * tp, 1), F32)]),
        out_shape=jax.ShapeDtypeStruct((b, n_heads, tp, hd), F32),
        compiler_params=_cparams("arbitrary", "arbitrary"), name="sb_sample",
    )(page_table.reshape(-1), q, new_t(col(1)), new_t(col(2)), *([cache_kt] * n_pg), *([cache_vt] * n_pg),
      bias_b, tri)
    return out[:, :, :t, :].transpose(0, 2, 1, 3).reshape(b * t, db)


def _merge_kernel(y_ref, r_ref, k_ref, v_ref, og_ref, yb_ref, ga_ref, gb_ref, wa_ref, wb_ref,
                  rk_ref, lg_ref, lb_ref, seg_ref, o_ref, ya_scr):
    @pl.when(pl.program_id(1) == 0)
    def _():
        seg = seg_ref[...]
        y = y_ref[...]
        hd_inv = 1.0 / jnp.sum(seg[0:1, :].astype(F32))
        mu = _head_sum(y, seg) * hd_inv
        yc = y - mu
        var = _head_sum(yc * yc, seg) * hd_inv
        yn = yc * lax.rsqrt(var + GN_EPS) * lg_ref[...] + lb_ref[...]
        bonus = _head_sum(r_ref[...] * k_ref[...] * rk_ref[...], seg)
        ya_scr[...] = ((yn + bonus * v_ref[...]) * og_ref[...]).astype(BF16)
    ma = jnp.dot(ya_scr[...], wa_ref[...], preferred_element_type=F32)
    mb = jnp.dot(yb_ref[...].astype(BF16), wb_ref[...], preferred_element_type=F32)
    o_ref[...] = (_sigmoid(ga_ref[...]) * ma + _sigmoid(gb_ref[...]) * mb).astype(BF16)


def _merge(y, rwkv_r, rwkv_k, rwkv_v, og, y_b, proj_main, gate_off, wa_bf, wb_bf, prm, seg, tm):
    n, da = y.shape
    db = y_b.shape[1]
    d = wa_bf.shape[1]
    tn = _tile(d, 512)
    ga_blk = gate_off // tn
    row = lambda w: pl.BlockSpec((tm, w), lambda i, j: (i, 0))
    vec = lambda w: pl.BlockSpec((1, w), lambda i, j: (0, 0))
    return pl.pallas_call(
        _merge_kernel, grid=(n // tm, d // tn),
        in_specs=[row(da)] * 5 + [row(db),
                  pl.BlockSpec((tm, tn), lambda i, j: (i, ga_blk + j)),
                  pl.BlockSpec((tm, tn), lambda i, j: (i, ga_blk + d // tn + j)),
                  pl.BlockSpec((da, tn), lambda i, j: (0, j)), pl.BlockSpec((db, tn), lambda i, j: (0, j)),
                  vec(da), vec(da), vec(da), pl.BlockSpec((LANES, LANES), lambda i, j: (0, 0))],
        out_specs=pl.BlockSpec((tm, tn), lambda i, j: (i, j)),
        out_shape=jax.ShapeDtypeStruct((n, d), BF16),
        scratch_shapes=[pltpu.VMEM((tm, da), BF16)],
        compiler_params=_cparams("arbitrary", "arbitrary"), name="merge",
    )(y, rwkv_r, rwkv_k, rwkv_v, og, y_b, proj_main, proj_main, wa_bf, wb_bf,
      prm["r_k"], prm["lnx_g"], prm["lnx_b"], seg)


def _wo_ln_kernel(alpha, m_ref, w_ref, x_ref, gm_ref, shf_ref, scf_ref, lg_ref, lb_ref, x1_o, h2_o):
    y = jnp.dot(m_ref[...], w_ref[...], preferred_element_type=F32)
    x1 = _std(alpha * x_ref[...] + gm_ref[...] * y, LN_EPS) * lg_ref[...] + lb_ref[...]
    x1_o[...] = x1
    h2_o[...] = (_std(x1, LN_EPS) * (1.0 + scf_ref[...]) + shf_ref[...]).astype(BF16)


def _wo_ln(merged, x, grp, mod, wo_bf, ln_g, ln_b, alpha, tm):
    n, d = x.shape
    row = pl.BlockSpec((tm, d), lambda i: (i, 0))
    vec = pl.BlockSpec((1, d), lambda i: (0, 0))
    return pl.pallas_call(
        functools.partial(_wo_ln_kernel, alpha), grid=(n // tm,),
        in_specs=[row, pl.BlockSpec((d, d), lambda i: (0, 0)), row,
                  _mod_spec(grp, 2, tm), _mod_spec(grp, 3, tm), _mod_spec(grp, 4, tm), vec, vec],
        out_specs=[row, row],
        out_shape=[jax.ShapeDtypeStruct((n, d), F32), jax.ShapeDtypeStruct((n, d), BF16)],
        compiler_params=_cparams("arbitrary"), name="wo_ln",
    )(merged, wo_bf, x, mod, mod, mod, ln_g, ln_b)


def _peer_q_kernel(ng, hk, h_ref, wq_ref, keys_ref, o_ref):
    qp = jnp.dot(h_ref[...], wq_ref[...], preferred_element_type=F32)
    for g in range(ng):
        o_ref[g] = _dot_nt(keys_ref[g], qp[:, g * hk:(g + 1) * hk].astype(BF16))


def _peer_q(h2, wq_bf, keys_bf, tm):
    n, d = h2.shape
    n_grp, nk, hk = keys_bf.shape
    ng = 4 if n_grp % 4 == 0 else 2
    return pl.pallas_call(
        functools.partial(_peer_q_kernel, ng, hk), grid=(n // tm, n_grp // ng),
        in_specs=[pl.BlockSpec((tm, d), lambda i, j: (i, 0)),
                  pl.BlockSpec((d, ng * hk), lambda i, j: (0, j)),
                  pl.BlockSpec((ng, nk, hk), lambda i, j: (j, 0, 0))],
        out_specs=pl.BlockSpec((ng, nk, tm), lambda i, j: (j, 0, i)),
        out_shape=jax.ShapeDtypeStruct((n_grp, nk, n), F32),
        compiler_params=_cparams("arbitrary", "arbitrary"), name="peer_q",
    )(h2, wq_bf, keys_bf)


def _top_rows(s, k, payloads=(), rid=None):
    if rid is None:
        rid = lax.broadcasted_iota(jnp.int32, s.shape, 0).astype(F32)
    vals, idxs, picked = [], [], [[] for _ in payloads]
    for _ in range(k):
        m = jnp.max(s, axis=0, keepdims=True)
        idx = jnp.min(jnp.where(s == m, rid, RID_NONE), axis=0, keepdims=True)
        hit = rid == idx
        vals.append(m)
        idxs.append(idx)
        for out, pay in zip(picked, payloads):
            out.append(jnp.max(jnp.where(hit, pay, -1.0), axis=0, keepdims=True))
        s = jnp.where(hit, -jnp.inf, s)
    return vals, idxs, picked


def _peer_topk_kernel(hp, s_ref, g_o, i1_o, i2_o):
    for hh in range(hp):
        rows = slice(hh * TOPK, (hh + 1) * TOPK)
        g_o[rows, :], i1_o[rows, :], i2_o[rows, :] = _peer_topk_head(s_ref[2 * hh], s_ref[2 * hh + 1])


def _peer_topk_head(s0, s1):
    tl = s0.shape[1]
    v0, i0, _ = _top_rows(s0, TOPK)
    v1, i1, _ = _top_rows(s1, TOPK)
    v0c, i0c = jnp.concatenate(v0, axis=0), jnp.concatenate(i0, axis=0)
    v1c, i1c = jnp.concatenate(v1, axis=0), jnp.concatenate(i1, axis=0)
    sub = lax.broadcasted_iota(jnp.int32, (SUBLANES, tl), 0)
    cand, rid, c1, c2 = [], [], [], []
    for i in range(TOPK // 2):
        nj = TOPK // (i + 1)
        for j0 in range(0, nj, SUBLANES):
            jj = sub + j0
            cand.append(jnp.where(jj < nj, v0[i] + v1c[j0:j0 + SUBLANES], -jnp.inf))
            rid.append((jj + i * TOPK).astype(F32))
            c1.append(jnp.broadcast_to(i0[i], (SUBLANES, tl)))
            c2.append(i1c[j0:j0 + SUBLANES])
    for i0_ in range(TOPK // 2, TOPK, SUBLANES):
        cand.append(v0c[i0_:i0_ + SUBLANES] + v1[0])
        rid.append(((sub + i0_) * TOPK).astype(F32))
        c1.append(i0c[i0_:i0_ + SUBLANES])
        c2.append(jnp.broadcast_to(i1[0], (SUBLANES, tl)))
    cat = lambda xs: jnp.concatenate(xs, axis=0)
    best, _, (e1, e2) = _top_rows(cat(cand), TOPK, (cat(c1), cat(c2)), rid=cat(rid))
    best = jnp.concatenate(best, axis=0)
    e = jnp.exp(best - best[0:1])
    return e / jnp.sum(e, axis=0, keepdims=True), jnp.concatenate(e1, axis=0), jnp.concatenate(e2, axis=0)


def _peer_topk(scores_t):
    n_grp, nk, n = scores_t.shape
    n_heads = n_grp // 2
    tl = _tile(n, LANES)
    hp = 4 if n_heads % 4 == 0 else 1
    out = jax.ShapeDtypeStruct((n_heads * TOPK, n), F32)
    ospec = pl.BlockSpec((hp * TOPK, tl), lambda i, h: (h, i))
    return pl.pallas_call(
        functools.partial(_peer_topk_kernel, hp), grid=(n // tl, n_heads // hp),
        in_specs=[pl.BlockSpec((2 * hp, nk, tl), lambda i, h: (h, 0, i))],
        out_specs=[ospec] * 3, out_shape=[out] * 3,
        compiler_params=_cparams("arbitrary", "arbitrary"), name="peer_topk",
    )(scores_t)


def _peer_gate_kernel(tg, nk, g_ref, i1_ref, i2_ref, o_ref, g_s, i1_s, i2_s):
    g_s[...] = g_ref[...].T
    i1_s[...] = i1_ref[...].T
    i2_s[...] = i2_ref[...].T
    m = g_s.shape[1]
    sub = lax.broadcasted_iota(jnp.int32, (nk, m), 0).astype(F32)
    steps = BF16_ROWS

    def body(nb, carry):
        rows = pl.ds(pl.multiple_of(nb * steps, steps), steps)
        g8, a8, b8 = g_s[rows, :], i1_s[rows, :], i2_s[rows, :]
        a_t = [jnp.where(a8[i:i + 1, :] == sub, g8[i:i + 1, :], 0.0).astype(BF16) for i in range(steps)]
        b_t = [jnp.where(b8[i:i + 1, :] == sub, 1.0, 0.0).astype(BF16) for i in range(steps)]
        g_tok = [_dot_nt(a_t[i], b_t[i]) for i in range(steps)]
        by_key = jnp.swapaxes(jnp.stack(g_tok, axis=0), 0, 1)
        for c in range(nk):
            o_ref[rows, c * nk:(c + 1) * nk] = by_key[c].astype(BF16)
        return carry

    lax.fori_loop(0, tg // steps, body, 0)


def _peer_gate(gate_t, i1_t, i2_t, nk):
    m, n = gate_t.shape
    tg = _tile(n, LANES)
    ispec = pl.BlockSpec((m, tg), lambda i: (0, i))
    return pl.pallas_call(
        functools.partial(_peer_gate_kernel, tg, nk), grid=(n // tg,),
        in_specs=[ispec] * 3,
        out_specs=pl.BlockSpec((tg, nk * nk), lambda i: (i, 0)),
        out_shape=jax.ShapeDtypeStruct((n, nk * nk), BF16),
        scratch_shapes=[pltpu.VMEM((tg, m), F32)] * 3,
        compiler_params=_cparams("arbitrary"), name="peer_gate",
    )(gate_t, i1_t, i2_t)


def _gelu_tanh(x):
    return 0.5 * x * (1.0 + jnp.tanh(math.sqrt(2.0 / math.pi) * (x + 0.044715 * (x * x * x))))


def _peer_dense_kernel(alpha, h_ref, g_ref, ut_ref, v_ref, x1_ref, gf_ref, lg_ref, lb_ref, o_ref):
    c = pl.program_id(1)

    @pl.when(c == 0)
    def _():
        o_ref[...] = jnp.zeros_like(o_ref)

    act = _gelu_tanh(jnp.dot(h_ref[...], ut_ref[...], preferred_element_type=F32))
    p = (act * g_ref[...].astype(F32)).astype(BF16)
    o_ref[...] += jnp.dot(p, v_ref[...], preferred_element_type=F32)

    @pl.when(c == pl.num_programs(1) - 1)
    def _():
        z = alpha * x1_ref[...] + gf_ref[...] * o_ref[...]
        o_ref[...] = _std(z, LN_EPS) * lg_ref[...] + lb_ref[...]


def _peer_dense(h2, gmat, ut_bf, v_bf, x1, grp, mod, ln_g, ln_b, alpha, tm):
    n, d = x1.shape
    e = v_bf.shape[0]
    te = _tile(e, 1024)
    row = pl.BlockSpec((tm, d), lambda i, c: (i, 0))
    vec = pl.BlockSpec((1, d), lambda i, c: (0, 0))
    return pl.pallas_call(
        functools.partial(_peer_dense_kernel, alpha), grid=(n // tm, e // te),
        in_specs=[row, pl.BlockSpec((tm, te), lambda i, c: (i, c)),
                  pl.BlockSpec((d, te), lambda i, c: (0, c)), pl.BlockSpec((te, d), lambda i, c: (c, 0)),
                  row, _mod_spec(grp, 5, tm), vec, vec],
        out_specs=row, out_shape=jax.ShapeDtypeStruct((n, d), F32),
        compiler_params=_cparams("arbitrary", "arbitrary"), name="peer_dense",
    )(h2, gmat, ut_bf, v_bf, x1, mod, ln_g, ln_b)


def _run_trunk(x, grp, mods, shift0, wkv0, cache_k, cache_v, page_table, w, dims):
    d, da, db, hd, lp, depth = dims["d"], dims["da"], dims["db"], dims["hd"], dims["lp"], dims["depth"]
    b, t = grp["b"], grp["t"]
    n = b * t
    tm = grp["tm"]
    alpha = (2 * depth) ** 0.25
    n_raw = 3 * da + dims["n_lora"]
    seg, diag = w["seg"], w["diag"]
    k_rows, v_rows, wkv_rows, shift_rows = [], [], [], []
    for l in range(depth):
        mod = mods[l]
        lw = w["layers"][l]
        proj_main = _proj(x, grp, mod, lw["w_main"], _tile(da, 1024))
        proj_lora = _proj(x, grp, mod, lw["w_lora"], lp)
        if grp["per_row"]:
            prev_main = jnp.repeat(shift0[l][:, :3 * da], t, axis=0)
            prev_lora = jnp.repeat(jnp.pad(shift0[l][:, 3 * da:], ((0, 0), (0, lp - dims["n_lora"]))), t, axis=0)
        else:
            prev_main = shift0[l][:, None, :3 * da]
            prev_lora = jnp.pad(shift0[l][:, None, 3 * da:], ((0, 0), (0, 0), (0, lp - dims["n_lora"])))
        r_, w_, k_, v_, a_, b_, og = _rwkv_prep(proj_main, proj_lora, grp, prev_main, prev_lora, lw, seg)
        if t % (LANES // (LANES // hd)) == 0:
            y, s_fin = _wkv_chunked((r_, w_, k_, v_, a_, b_), wkv0[l], grp, hd)
        else:
            y, s_fin = _scan((r_, w_, k_, v_, a_, b_), wkv0[l], grp, seg, diag)
        if cache_k is None:
            y_b = _sb_prompt(proj_main, grp, lw["sb_bias"], db, hd, 3 * da)
        else:
            y_b = _sb_sample(proj_main, grp, cache_k, cache_v, l, page_table, lw["sb_bias"], db, hd, 3 * da)
        merged = _merge(y, r_, k_, v_, og, y_b, proj_main, 3 * da + 3 * db, lw["w_br_a"], lw["w_br_b"], lw, seg,
                        grp["tm"])
        x1, h2 = _wo_ln(merged, x, grp, mod, lw["wo"], lw["ln1_g"], lw["ln1_b"], alpha, grp["tm_prep"])
        scores_t = _peer_q(h2, lw["wq"], lw["keys"], tm)
        gate_t, i1_t, i2_t = _peer_topk(scores_t)
        nk = lw["keys"].shape[1]
        gmat = _peer_gate(gate_t, i1_t, i2_t, nk)
        x = _peer_dense(h2, gmat, lw["ut"], lw["v"], x1, grp, mod, lw["ln2_g"], lw["ln2_b"], alpha, tm)
        p3 = proj_main.reshape(b, t, -1)
        k_rows.append(p3[:, :, 3 * da + db:3 * da + 2 * db].reshape(b, t, db // hd, hd))
        v_rows.append(p3[:, :, 3 * da + 2 * db:3 * da + 3 * db].reshape(b, t, db // hd, hd))
        wkv_rows.append(s_fin)
        shift_rows.append(jnp.concatenate(
            [p3[:, -1, :3 * da], proj_lora.reshape(b, t, lp)[:, -1, :dims["n_lora"]]], axis=-1))
        assert shift_rows[-1].shape[-1] == n_raw
    return (x.reshape(b, t, d), jnp.stack(k_rows), jnp.stack(v_rows), jnp.stack(wkv_rows), jnp.stack(shift_rows))


def kernel(x_prompt, x_sample, cache_k, cache_v, state_wkv, state_shift, page_table, c_prompt, c_sample,
           w_ada, b_ada, w_in, mu_shift, decay_bias, decay_up, aaa_bias, aaa_up, og_up, k_k, k_a, r_k,
           lnx_g, lnx_b, sb_bias, w_branch_a, w_branch_b, w_o, ln1_g, ln1_b, peer_wq, peer_keys, peer_u, peer_v,
           ln2_g, ln2_b):
    depth, d, _ = w_ada.shape
    hd = cache_k.shape[-1]
    da = decay_up.shape[2]
    db = w_branch_b.shape[1]
    wl, al, gl = decay_up.shape[1], aaa_up.shape[1], og_up.shape[1]
    n_lora = wl + al + gl
    lp = -(-n_lora // LANES) * LANES
    assert da % LANES == 0 and db % LANES == 0 and LANES % hd == 0
    dims = dict(d=d, da=da, db=db, hd=hd, lp=lp, depth=depth, n_lora=n_lora)

    seg = (jnp.arange(LANES)[:, None] // hd == jnp.arange(LANES)[None, :] // hd).astype(BF16)
    diag = (jnp.arange(hd)[:, None] == jnp.arange(LANES)[None, :] % hd).astype(F32)
    layers = []
    for l in range(depth):
        rw = 3 * da + n_lora
        pad_rows = lambda m, off: jnp.pad(m, ((off, lp - off - m.shape[0]), (0, 0))).astype(BF16)
        n_grp = peer_keys.shape[1] * peer_keys.shape[2]
        layers.append(dict(
            da=da, lp=lp,
            w_main=jnp.concatenate([w_in[l][:, :3 * da], w_in[l][:, rw:]], axis=1).astype(BF16),
            w_lora=jnp.pad(w_in[l][:, 3 * da:rw], ((0, 0), (0, lp - n_lora))).astype(BF16),
            mu_main=mu_shift[l][None, :3 * da],
            mu_lora=jnp.pad(mu_shift[l][None, 3 * da:], ((0, 0), (0, lp - n_lora))),
            decay_bias=decay_bias[l][None], aaa_bias=aaa_bias[l][None],
            wd=pad_rows(decay_up[l], 0), wa=pad_rows(aaa_up[l], wl), wg=pad_rows(og_up[l], wl + al),
            w_br_a=w_branch_a[l].astype(BF16), w_br_b=w_branch_b[l].astype(BF16),
            k_k=k_k[l][None], k_a=k_a[l][None], r_k=r_k[l].reshape(1, da),
            lnx_g=lnx_g[l][None], lnx_b=lnx_b[l][None], sb_bias=sb_bias[l],
            wo=w_o[l].astype(BF16), ln1_g=ln1_g[l][None], ln1_b=ln1_b[l][None],
            wq=peer_wq[l].astype(BF16), keys=peer_keys[l].reshape(n_grp, *peer_keys.shape[3:]).astype(BF16),
            ut=peer_u[l].T.astype(BF16), v=peer_v[l].astype(BF16),
            ln2_g=ln2_g[l][None], ln2_b=ln2_b[l][None]))
    w = dict(seg=seg, diag=diag, layers=layers)

    bp, tp_, _ = x_prompt.shape
    bs, ts, _ = x_sample.shape
    mod_all = _ada(jnp.concatenate([c_prompt, c_sample], axis=0), w_ada, b_ada)

    def group(b, t):
        n = b * t
        tm = _tile(n, 512) if t < 512 else _tile(t, 512)
        tm_prep = _tile(n, 256) if t < 256 else _tile(t, 256)
        tm_big = _tile(n, 1024) if t < tm else _tile(t, 1024)
        return dict(b=b, t=t, d=d, tm=tm, tm_prep=tm_prep, tm_big=tm_big, per_row=t < tm)

    gp, gs = group(bp, tp_), group(bs, ts)
    assert gp["per_row"] == (gp["t"] < gp["tm_prep"]) and gs["per_row"] == (gs["t"] < gs["tm_prep"])

    def mods_for(grp, lo, hi):
        out = []
        for l in range(depth):
            m = mod_all[l, lo:hi]
            out.append(jnp.repeat(m, grp["t"], axis=0) if grp["per_row"] else m.reshape(hi - lo, 6, 1, d))
        return out

    shift0_p = jnp.zeros((depth, bp, 3 * da + n_lora), F32)
    wkv0_p = jnp.zeros((depth, bp, da // hd, hd, hd), F32)
    y_p, k_p, v_p, wkv_p, shift_p = _run_trunk(
        x_prompt.reshape(bp * tp_, d), gp, mods_for(gp, 0, bp), shift0_p, wkv0_p, None, None, None, w, dims)
    y_s, k_s, v_s, wkv_s, shift_s = _run_trunk(
        x_sample.reshape(bs * ts, d), gs, mods_for(gs, bp, bp + bs), state_shift, state_wkv,
        cache_k.transpose(0, 1, 3, 4, 2), cache_v.transpose(0, 1, 3, 4, 2), page_table, w, dims)
    return (y_p, y_s, k_p, v_p, wkv_p, shift_p, k_s, v_s, wkv_s, shift_s)
```

```python
import functools
import math

import jax
import jax.numpy as jnp
from jax import lax
from jax.experimental import pallas as pl
from jax.experimental.pallas import tpu as pltpu

F32 = jnp.float32
BF16 = jnp.bfloat16
LN_EPS = 1e-5
GN_EPS = 64e-5
KK_EPS = 1e-12
TOPK = 16
LANES = 128
SUBLANES = 8
BF16_ROWS = 16
RID_NONE = 1e9
SB_UNROLL = 2
VMEM_LIMIT = 56 * 1024 * 1024


def _cparams(*sem):
    return pltpu.CompilerParams(dimension_semantics=sem, vmem_limit_bytes=VMEM_LIMIT)


def _tile(n, pref):
    if n <= pref:
        return n
    t = (pref // LANES) * LANES
    while t > LANES and n % t:
        t -= LANES
    assert n % t == 0, (n, pref)
    return t


def _std(x, eps):
    mu = jnp.mean(x, axis=-1, keepdims=True)
    xc = x - mu
    var = jnp.mean(xc * xc, axis=-1, keepdims=True)
    return xc * lax.rsqrt(var + eps)


def _sigmoid(x):
    return 1.0 / (1.0 + jnp.exp(-x))


def _softplus(x):
    return jnp.maximum(x, 0.0) + jnp.log(1.0 + jnp.exp(-jnp.abs(x)))


def _split_dot(x, m, *, left=False):
    hi = x.astype(BF16)
    lo = (x - hi.astype(F32)).astype(BF16)
    if left:
        return (jnp.dot(m, hi, preferred_element_type=F32) + jnp.dot(m, lo, preferred_element_type=F32))
    return (jnp.dot(hi, m, preferred_element_type=F32) + jnp.dot(lo, m, preferred_element_type=F32))


def _head_sum(x, seg):
    outs = [_split_dot(x[:, c * LANES:(c + 1) * LANES], seg) for c in range(x.shape[1] // LANES)]
    return outs[0] if len(outs) == 1 else jnp.concatenate(outs, axis=1)


def _dot_nt(a, b):
    return lax.dot_general(a, b, (((1,), (1,)), ((), ())), preferred_element_type=F32)


def _dot_tn(a, b):
    return lax.dot_general(a, b, (((0,), (0,)), ((), ())), preferred_element_type=F32)


def _ada_kernel(c_ref, w_ref, b_ref, o_ref):
    c = c_ref[...]
    s = (c * _sigmoid(c)).astype(BF16)
    o_ref[...] = jnp.dot(s, w_ref[...].astype(BF16), preferred_element_type=F32) + b_ref[...]


def _ada(c_all, w_ada, b_ada):
    depth, d, d6 = w_ada.shape
    nb = c_all.shape[0]
    tn = _tile(d6, 1024)
    return pl.pallas_call(
        _ada_kernel, grid=(depth, d6 // tn),
        in_specs=[pl.BlockSpec((nb, d), lambda l, j: (0, 0)),
                  pl.BlockSpec((None, d, tn), lambda l, j: (l, 0, j)),
                  pl.BlockSpec((None, 1, tn), lambda l, j: (l, 0, j))],
        out_specs=pl.BlockSpec((None, nb, tn), lambda l, j: (l, 0, j)),
        out_shape=jax.ShapeDtypeStruct((depth, nb, d6), F32),
        compiler_params=_cparams("arbitrary", "arbitrary"), name="ada",
    )(c_all, w_ada, b_ada.reshape(depth, 1, d6))


def _mod_spec(grp, which, tm):
    d = grp["d"]
    if grp["per_row"]:
        return pl.BlockSpec((tm, d), lambda i, *_: (i, which))
    t = grp["t"]
    return pl.BlockSpec((None, None, 1, d), lambda i, *_: ((i * tm) // t, which, 0, 0))


def _proj_kernel(x_ref, sh_ref, sc_ref, w_ref, o_ref, h_scr):
    @pl.when(pl.program_id(1) == 0)
    def _():
        tm = x_ref.shape[0]
        rc = min(tm, 256)

        def chunk(c, carry):
            rows = pl.ds(pl.multiple_of(c * rc, rc), rc)
            vec = lambda ref: ref[...] if ref.shape[0] == 1 else ref[rows, :]
            h = _std(x_ref[rows, :], LN_EPS) * (1.0 + vec(sc_ref)) + vec(sh_ref)
            h_scr[rows, :] = h.astype(BF16)
            return carry

        lax.fori_loop(0, tm // rc, chunk, 0)
    o_ref[...] = jnp.dot(h_scr[...], w_ref[...], preferred_element_type=F32)


def _proj(x, grp, mod, w_bf, tn):
    n, d = x.shape
    p = w_bf.shape[1]
    tm = grp["tm_big"]
    return pl.pallas_call(
        _proj_kernel, grid=(n // tm, p // tn),
        in_specs=[pl.BlockSpec((tm, d), lambda i, j: (i, 0)),
                  _mod_spec(grp, 0, tm), _mod_spec(grp, 1, tm),
                  pl.BlockSpec((d, tn), lambda i, j: (0, j))],
        out_specs=pl.BlockSpec((tm, tn), lambda i, j: (i, j)),
        out_shape=jax.ShapeDtypeStruct((n, p), F32),
        scratch_shapes=[pltpu.VMEM((tm, d), BF16)],
        compiler_params=_cparams("arbitrary", "arbitrary"), name="proj",
    )(x, mod, mod, w_bf)


def _rwkv_prep_kernel(t_seq, tm, da, per_row, pm_ref, pl_ref, prevm_ref, prevl_ref, mum_ref, mul_ref,
                      dbias_ref, abias_ref, wd_ref, wa_ref, wg_ref, kk_ref, ka_ref, seg_ref,
                      r_o, w_o, k_o, v_o, a_o, b_o, og_o, carm, carl):
    i = pl.program_id(0)
    row = lax.broadcasted_iota(jnp.int32, (tm, 1), 0)

    def shifted(p, prev_ref, car):
        rolled = pltpu.roll(p, 1, 0)
        if per_row:
            return jnp.where(lax.rem(row, t_seq) == 0, prev_ref[...], rolled)
        first = jnp.where(lax.rem(i, t_seq // tm) == 0, prev_ref[...], car[...])
        car[...] = p[tm - 1:tm, :]
        return jnp.where(row == 0, first, rolled)

    p = pm_ref[...]
    pm = p + (shifted(p, prevm_ref, carm) - p) * mum_ref[...]
    q = pl_ref[...]
    lo = q + (shifted(q, prevl_ref, carl) - q) * mul_ref[...]
    r = pm[:, :da]
    k = pm[:, da:2 * da]
    v = pm[:, 2 * da:3 * da]
    w_pre = dbias_ref[...] + jnp.dot(jnp.tanh(lo).astype(BF16), wd_ref[...], preferred_element_type=F32)
    decay = jnp.exp(-math.exp(-0.5) * _sigmoid(w_pre))
    a = _sigmoid(abias_ref[...] + jnp.dot(lo.astype(BF16), wa_ref[...], preferred_element_type=F32))
    og = jnp.dot(_sigmoid(lo).astype(BF16), wg_ref[...], preferred_element_type=F32)
    kk = k * kk_ref[...]
    kk = kk * lax.rsqrt(_head_sum(kk * kk, seg_ref[...]) + KK_EPS)
    r_o[...] = r
    w_o[...] = decay
    k_o[...] = k * (1.0 + (a - 1.0) * ka_ref[...])
    v_o[...] = v
    a_o[...] = -kk
    b_o[...] = kk * a
    og_o[...] = og


def _rwkv_prep(proj_main, proj_lora, grp, prev_main, prev_lora, prm, seg):
    n = proj_main.shape[0]
    da, lp = prm["da"], prm["lp"]
    tm = grp["tm_prep"]
    per_row = grp["t"] < tm
    row_spec = lambda w: pl.BlockSpec((tm, w), lambda i: (i, 0))
    vec_spec = lambda w: pl.BlockSpec((1, w), lambda i: (0, 0))
    if per_row:
        prev_specs = [row_spec(3 * da), row_spec(lp)]
    else:
        t = grp["t"]
        prev_specs = [pl.BlockSpec((None, 1, 3 * da), lambda i: ((i * tm) // t, 0, 0)),
                      pl.BlockSpec((None, 1, lp), lambda i: ((i * tm) // t, 0, 0))]
    out = jax.ShapeDtypeStruct((n, da), F32)
    return pl.pallas_call(
        functools.partial(_rwkv_prep_kernel, grp["t"], tm, da, per_row), grid=(n // tm,),
        in_specs=[row_spec(3 * da), row_spec(lp)] + prev_specs + [
            vec_spec(3 * da), vec_spec(lp), vec_spec(da), vec_spec(da),
            pl.BlockSpec((lp, da), lambda i: (0, 0)), pl.BlockSpec((lp, da), lambda i: (0, 0)),
            pl.BlockSpec((lp, da), lambda i: (0, 0)), vec_spec(da), vec_spec(da),
            pl.BlockSpec((LANES, LANES), lambda i: (0, 0))],
        out_specs=[row_spec(da)] * 7, out_shape=[out] * 7,
        scratch_shapes=[pltpu.VMEM((1, 3 * da), F32), pltpu.VMEM((1, lp), F32)],
        compiler_params=_cparams("arbitrary"), name="rwkv_prep",
    )(proj_main, proj_lora, prev_main, prev_lora, prm["mu_main"], prm["mu_lora"], prm["decay_bias"],
      prm["aaa_bias"], prm["wd"], prm["wa"], prm["wg"], prm["k_k"], prm["k_a"], seg)


def _scan_kernel(tt, n_pairs, bb, r_ref, w_ref, k_ref, v_ref, a_ref, b_ref, s0_ref, seg_ref, dm_ref,
                 y_ref, sT_ref, s_scr):
    tc = pl.program_id(1)

    @pl.when(tc == 0)
    def _():
        s_scr[...] = s0_ref[...]

    seg = seg_ref[...]
    diag = dm_ref[...]
    steps = min(tt, SUBLANES)
    chains = [(s, p) for s in range(bb) for p in range(n_pairs)]
    lanes = lambda p: slice(p * LANES, (p + 1) * LANES)

    def block(tb, carry):
        rows = pl.ds(pl.multiple_of(tb * steps, steps), steps)
        r8, w8, k8, v8, a8, b8 = [[ref[s, rows, :] for s in range(bb)]
                                  for ref in (r_ref, w_ref, k_ref, v_ref, a_ref, b_ref)]
        ys = [[] for _ in range(bb)]
        for i in range(steps):
            row = lambda x, c: x[c[0]][i:i + 1, lanes(c[1])]
            st = [s_scr[s, :, lanes(p)] for s, p in chains]
            sa = [_split_dot(st[n] * row(a8, c), seg) for n, c in enumerate(chains)]
            vcol = [_split_dot(diag * row(v8, c), seg) for c in chains]
            st = [st[n] * row(w8, c) + sa[n] * row(b8, c) + vcol[n] * row(k8, c) for n, c in enumerate(chains)]
            for n, (s, p) in enumerate(chains):
                s_scr[s, :, lanes(p)] = st[n]
            yb = [_split_dot(st[n] * row(r8, c), seg) for n, c in enumerate(chains)]
            yv = [jnp.sum(x * diag, axis=0, keepdims=True) for x in yb]
            for s in range(bb):
                part = yv[s * n_pairs:(s + 1) * n_pairs]
                ys[s].append(part[0] if n_pairs == 1 else jnp.concatenate(part, axis=1))
        for s in range(bb):
            y_ref[s, rows, :] = jnp.concatenate(ys[s], axis=0)
        return carry

    lax.fori_loop(0, tt // steps, block, 0)

    @pl.when(tc == pl.num_programs(1) - 1)
    def _():
        sT_ref[...] = s_scr[...]


def _wkv_chunk_kernel(n_sub, n_pairs, c, hd, r_ref, w_ref, k_ref, v_ref, a_ref, b_ref, h0_ref, y_ref, hT_ref, h_scr):
    tc = pl.program_id(1)
    nh = LANES // hd
    rows_st = nh * c

    @pl.when(tc == 0)
    def _():
        h_scr[...] = h0_ref[...]

    lane_head = lax.broadcasted_iota(jnp.int32, (1, LANES), 1) // hd
    ri = lax.broadcasted_iota(jnp.int32, (rows_st, rows_st), 0)
    ci = lax.broadcasted_iota(jnp.int32, (rows_st, rows_st), 1)
    same_head = (ri // c) == (ci // c)
    strict = same_head & (ci < ri)
    incl = same_head & (ci <= ri)
    ti = lax.broadcasted_iota(jnp.int32, (c, c), 0)
    tj = lax.broadcasted_iota(jnp.int32, (c, c), 1)
    tri_incl = jnp.where(tj <= ti, 1.0, 0.0).astype(BF16)
    ones_3c = jnp.ones((3 * c, LANES), BF16)

    def stack(x):
        return jnp.concatenate([jnp.where(lane_head == h, x, 0.0) for h in range(nh)], axis=0).astype(BF16)

    def dot(a, b):
        return jnp.dot(a, b, preferred_element_type=F32)

    def sub_chunk(sc, carry):
        rows = pl.ds(pl.multiple_of(sc * c, c), c)
        pairs = range(n_pairs)
        sls = [slice(p * LANES, (p + 1) * LANES) for p in pairs]
        ld = lambda ref: [ref[rows, sl] for sl in sls]
        r, w, k, v, a, b = ld(r_ref), ld(w_ref), ld(k_ref), ld(v_ref), ld(a_ref), ld(b_ref)
        lw = [jnp.log(x) for x in w]
        cum = [_split_dot(x, tri_incl, left=True) for x in lw]
        p_inv = [jnp.exp(-x) for x in cum]
        a_st = [stack(a[p] * jnp.exp(cum[p] - lw[p])) for p in pairs]
        r_st = [stack(r[p] * jnp.exp(cum[p])) for p in pairs]
        b_st = [stack(b[p] * p_inv[p]) for p in pairs]
        k_st = [stack(k[p] * p_inv[p]) for p in pairs]
        v_st = [stack(x) for x in v]
        h = [h_scr[p] for p in pairs]
        hb = [x.astype(BF16) for x in h]
        cat0 = lambda *xs: jnp.concatenate(xs, axis=0)
        cat1 = lambda *xs: jnp.concatenate(xs, axis=1)
        cross = [_dot_nt(cat0(a_st[p], r_st[p]), cat0(b_st[p], k_st[p])) for p in pairs]
        npow = [jnp.where(strict, x[:rows_st, :rows_st], 0.0).astype(BF16) for x in cross]
        n_ak = [jnp.where(strict, x[:rows_st, rows_st:], 0.0).astype(BF16) for x in cross]
        m_rb = [jnp.where(incl, x[rows_st:, :rows_st], 0.0).astype(BF16) for x in cross]
        m_rk = [jnp.where(incl, x[rows_st:, rows_st:], 0.0).astype(BF16) for x in cross]
        u = [dot(cat1(a_st[p], n_ak[p]), cat0(hb[p], v_st[p])) for p in pairs]
        span = 1
        while span < c:
            span *= 2
            if span < c:
                both = [dot(npow[p], cat1(u[p].astype(BF16), npow[p])) for p in pairs]
                u = [u[p] + both[p][:, :LANES] for p in pairs]
                npow = [x[:, LANES:].astype(BF16) for x in both]
            else:
                u = [u[p] + dot(npow[p], u[p].astype(BF16)) for p in pairs]
        ub = [x.astype(BF16) for x in u]
        for p in pairs:
            y_st = dot(cat1(r_st[p], m_rb[p], m_rk[p]), cat0(hb[p], ub[p], v_st[p]))
            y = y_st[0:c]
            for hh in range(1, nh):
                y = y + y_st[hh * c:(hh + 1) * c]
            y_ref[rows, sls[p]] = y
        for p in pairs:
            l1 = lw[p].astype(BF16)
            r1 = lw[p] - l1.astype(F32)
            l2 = r1.astype(BF16)
            l3 = (r1 - l2.astype(F32)).astype(BF16)
            tot_col = _dot_tn(cat0(l1, l2, l3), ones_3c)
            p_end = jnp.exp(cum[p][c - 1:c, :] - cum[p])
            h_scr[p] = h[p] * jnp.exp(tot_col) + _dot_tn(cat0(stack(b[p] * p_end), stack(k[p] * p_end)),
                                                         cat0(ub[p], v_st[p]))
        return carry

    lax.fori_loop(0, n_sub, sub_chunk, 0)

    @pl.when(tc == pl.num_programs(1) - 1)
    def _():
        hT_ref[...] = h_scr[...]


def _wkv_chunked(rwkv, wkv0, grp, hd):
    b, t = grp["b"], grp["t"]
    da = rwkv[0].shape[1]
    nh = LANES // hd
    n_pairs = da // LANES
    c = LANES // nh
    tt = min(t, 256)
    eye = jnp.eye(nh, dtype=F32)
    h0 = wkv0.transpose(0, 1, 3, 2).reshape(b, n_pairs, nh, hd, 1, hd) * eye[None, None, :, None, :, None]
    h0 = h0.reshape(b, n_pairs, LANES, LANES)
    seq_spec = pl.BlockSpec((None, tt, da), lambda i, j: (i, j, 0))
    st_spec = pl.BlockSpec((None, n_pairs, LANES, LANES), lambda i, j: (i, 0, 0, 0))
    y, h_fin = pl.pallas_call(
        functools.partial(_wkv_chunk_kernel, tt // c, n_pairs, c, hd), grid=(b, t // tt),
        in_specs=[seq_spec] * 6 + [st_spec], out_specs=[seq_spec, st_spec],
        out_shape=[jax.ShapeDtypeStruct((b, t, da), F32), jax.ShapeDtypeStruct((b, n_pairs, LANES, LANES), F32)],
        scratch_shapes=[pltpu.VMEM((n_pairs, LANES, LANES), F32)],
        compiler_params=_cparams("arbitrary", "arbitrary"), name="wkv_chunked",
    )(*[x.reshape(b, t, da) for x in rwkv], h0)
    h6 = h_fin.reshape(b, n_pairs, nh, hd, nh, hd)
    s_fin = jnp.stack([h6[:, :, h, :, h, :] for h in range(nh)], axis=2)
    return y.reshape(b * t, da), s_fin.reshape(b, n_pairs * nh, hd, hd).transpose(0, 1, 3, 2)


def _scan(rwkv, wkv0, grp, seg, diag):
    b, t = grp["b"], grp["t"]
    da = rwkv[0].shape[1]
    hd = wkv0.shape[-1]
    s0 = wkv0.transpose(0, 2, 1, 3).reshape(b, hd, da)
    tt = min(t, 256)
    bb = 4 if (t <= SUBLANES and b % 4 == 0) else 1
    seq_spec = pl.BlockSpec((bb, tt, da), lambda i, j: (i, j, 0))
    st_spec = pl.BlockSpec((bb, hd, da), lambda i, j: (i, 0, 0))
    y, s_fin = pl.pallas_call(
        functools.partial(_scan_kernel, tt, da // LANES, bb), grid=(b // bb, t // tt),
        in_specs=[seq_spec] * 6 + [st_spec, pl.BlockSpec((LANES, LANES), lambda i, j: (0, 0)),
                                   pl.BlockSpec((hd, LANES), lambda i, j: (0, 0))],
        out_specs=[seq_spec, st_spec],
        out_shape=[jax.ShapeDtypeStruct((b, t, da), F32), jax.ShapeDtypeStruct((b, hd, da), F32)],
        scratch_shapes=[pltpu.VMEM((bb, hd, da), F32)],
        compiler_params=_cparams("arbitrary", "arbitrary"), name="wkv_scan",
    )(*[x.reshape(b, t, da) for x in rwkv], s0, seg, diag)
    return y.reshape(b * t, da), s_fin.reshape(b, hd, da // hd, hd).transpose(0, 2, 1, 3)


def _sb_prompt_kernel(tq, hd, scale, bias_ref, q_ref, k_ref, v_ref, tri_ref, ones_ref, o_ref):
    p, i = pl.program_id(1), pl.program_id(2)
    nh = LANES // hd
    lane = lax.broadcasted_iota(jnp.int32, (1, LANES), 1)
    hmasks = [(lane >= hh * hd) & (lane < (hh + 1) * hd) for hh in range(nh)]
    q = q_ref[...] * scale
    qhs = [jnp.where(hm, q, 0.0).astype(BF16) for hm in hmasks]
    biases = [bias_ref[p * nh + hh] for hh in range(nh)]
    tri = tri_ref[...]

    def key_blocks(js, carry, vis):
        acc, runs = carry
        rows = [pl.ds(pl.multiple_of(j * tq, tq), tq) for j in js]
        kbs = [k_ref[r, :].astype(BF16) for r in rows]
        vs = [v_ref[r, :] for r in rows]
        chains = [(g, hh) for g in range(len(js)) for hh in range(nh)]
        nch = len(chains)
        runs = list(runs)
        zs, es, cums, tots, wts = [None] * nch, [None] * nch, [None] * nch, [None] * nch, [None] * nch

        def scores(n):
            g, hh = chains[n]
            zs[n] = _dot_nt(qhs[hh], kbs[g]) + biases[hh]

        def front(n):
            sp = _softplus(zs[n])
            es[n] = zs[n] - sp
            x = (sp if vis is None else jnp.where(vis, sp, 0.0)).astype(BF16)
            cums[n] = jnp.dot(x, tri, preferred_element_type=F32)
            tots[n] = jnp.dot(x, ones_ref[...], preferred_element_type=F32)[:, 0:1]

        def back(n):
            hh = chains[n][1]
            w = jnp.exp(es[n] - cums[n] - runs[hh])
            wts[n] = (w if vis is None else jnp.where(vis, w, 0.0)).astype(BF16)
            runs[hh] = runs[hh] + tots[n]

        for n in range(min(2, nch)):
            scores(n)
        for n in range(nch):
            front(n)
            if n + 2 < nch:
                scores(n + 2)
            if n >= 1:
                back(n - 1)
        back(nch - 1)
        vhs = [jnp.where(hmasks[hh], vs[g], 0.0).astype(BF16) for g, hh in chains]
        acc = acc + jnp.dot(jnp.concatenate(wts, axis=1), jnp.concatenate(vhs, axis=0),
                            preferred_element_type=F32)
        return acc, tuple(runs)

    qrow = lax.broadcasted_iota(jnp.int32, (tq, 1), 0)
    kcol = lax.broadcasted_iota(jnp.int32, (1, tq), 1)
    carry = (jnp.zeros((tq, LANES), F32), tuple(jnp.zeros((tq, 1), F32) for _ in range(nh)))
    carry = key_blocks([i], carry, kcol < qrow)
    carry = lax.fori_loop(
        0, i // SB_UNROLL,
        lambda s, c: key_blocks([i - 1 - SB_UNROLL * s - g for g in range(SB_UNROLL)], c, None), carry)
    rem = lax.rem(i, SB_UNROLL)
    carry = lax.fori_loop(0, rem, lambda s, c: key_blocks([rem - 1 - s], c, None), carry)
    o_ref[...] = carry[0]


def _sb_prompt(proj_main, grp, sb_bias, db, hd, q_off):
    b, t = grp["b"], grp["t"]
    n = proj_main.shape[0]
    tq = min(t, 256)
    nq = t // tq
    cb = q_off // LANES
    tri = (jnp.arange(tq)[:, None] > jnp.arange(tq)[None, :]).astype(BF16)
    return pl.pallas_call(
        functools.partial(_sb_prompt_kernel, tq, hd, hd ** -0.5),
        grid=(b, db // LANES, nq),
        in_specs=[pl.BlockSpec(memory_space=pltpu.SMEM),
                  pl.BlockSpec((tq, LANES), lambda bb, p, i: (bb * nq + i, cb + p)),
                  pl.BlockSpec((t, LANES), lambda bb, p, i: (bb, cb + db // LANES + p)),
                  pl.BlockSpec((t, LANES), lambda bb, p, i: (bb, cb + 2 * (db // LANES) + p)),
                  pl.BlockSpec((tq, tq), lambda bb, p, i: (0, 0)),
                  pl.BlockSpec((tq, LANES), lambda bb, p, i: (0, 0))],
        out_specs=pl.BlockSpec((tq, LANES), lambda bb, p, i: (bb * nq + i, p)),
        out_shape=jax.ShapeDtypeStruct((n, db), F32),
        compiler_params=_cparams("arbitrary", "arbitrary", "arbitrary"), name="sb_prompt",
    )(sb_bias, proj_main, proj_main, proj_main, tri, jnp.ones((tq, LANES), BF16))


def _sb_sample_kernel(n_pg, n_heads, tp, page, scale, pt_ref, q_ref, kn_ref, vn_ref, *rest):
    kc_refs, vc_refs = rest[:n_pg], rest[n_pg:2 * n_pg]
    bias_ref, tri_ref, o_ref, qbd_scr, kn_scr, vn_scr, acc_scr, run_scr = rest[2 * n_pg:]
    j = pl.program_id(1)
    rows = n_heads * tp
    tri = tri_ref[...]
    hd = q_ref.shape[2]
    flat = lambda x: x.reshape(n_heads * hd, page).astype(BF16)

    @pl.when(j == 0)
    def _():
        qs = q_ref[...] * scale
        qbd_scr[...] = jnp.zeros_like(qbd_scr)
        for h in range(n_heads):
            qbd_scr[h * tp:(h + 1) * tp, h * hd:(h + 1) * hd] = qs[h].astype(BF16)
        kn_scr[...] = jnp.zeros_like(kn_scr)
        vn_scr[...] = jnp.zeros_like(vn_scr)
        kn_scr[:, 0:tp, :] = kn_ref[...]
        vn_scr[:, 0:tp, :] = vn_ref[...]
        z = lax.dot_general(qs.astype(BF16), kn_scr[...].astype(BF16), (((2,), (2,)), ((0,), (0,))),
                            preferred_element_type=F32).reshape(rows, page) + bias_ref[...]
        vis = (lax.broadcasted_iota(jnp.int32, (rows, page), 1)
               < lax.rem(lax.broadcasted_iota(jnp.int32, (rows, page), 0), tp))
        sp = _softplus(z)
        nlk = jnp.where(vis, sp, 0.0)
        cum = jnp.dot(nlk.astype(BF16), tri, preferred_element_type=F32)
        w = jnp.where(vis, jnp.exp(z - sp - cum), 0.0).astype(BF16).reshape(n_heads, tp, page)
        out = lax.dot_general(w, vn_scr[...].astype(BF16), (((2,), (1,)), ((0,), (0,))),
                              preferred_element_type=F32)
        acc_scr[...] = jnp.zeros_like(acc_scr)
        for h in range(n_heads):
            acc_scr[h * tp:(h + 1) * tp, h * hd:(h + 1) * hd] = out[h]
        run_scr[...] = jnp.sum(nlk, axis=1, keepdims=True)

    z_all = jnp.dot(qbd_scr[...], jnp.concatenate([flat(ref[...]) for ref in kc_refs], axis=1),
                    preferred_element_type=F32)
    zs = [z_all[:, g * page:(g + 1) * page] + bias_ref[...] for g in range(n_pg)]
    sps = [_softplus(z) for z in zs]
    cums = [jnp.dot(sp.astype(BF16), tri, preferred_element_type=F32) for sp in sps]
    run = run_scr[...]
    wts = []
    for g in range(n_pg):
        wts.append(jnp.exp(zs[g] - sps[g] - cums[g] - run).astype(BF16))
        run = run + jnp.sum(sps[g], axis=1, keepdims=True)
    run_scr[...] = run
    acc_scr[...] += _dot_nt(jnp.concatenate(wts, axis=1),
                            jnp.concatenate([flat(ref[...]) for ref in vc_refs], axis=1))

    @pl.when(j == pl.num_programs(1) - 1)
    def _():
        for h in range(n_heads):
            o_ref[h] = acc_scr[h * tp:(h + 1) * tp, h * hd:(h + 1) * hd]


def _sb_sample(proj_main, grp, cache_kt, cache_vt, layer, page_table, sb_bias, db, hd, q_off):
    b, t = grp["b"], grp["t"]
    n_heads = db // hd
    page = cache_kt.shape[-1]
    n_pages = page_table.shape[1]
    tp = SUBLANES
    assert t <= tp
    n_pg = next(g for g in (16, 8, 4, 2, 1) if n_pages % g == 0)
    col = lambda i: lax.slice_in_dim(proj_main, q_off + i * db, q_off + (i + 1) * db, axis=1).reshape(b, t, n_heads, hd)
    by_head = lambda x: jnp.pad(x.transpose(0, 2, 1, 3), ((0, 0), (0, 0), (0, tp - t), (0, 0)))
    bias_b = jnp.broadcast_to(jnp.repeat(sb_bias, tp)[:, None], (n_heads * tp, page))
    tri = (jnp.arange(page)[:, None] > jnp.arange(page)[None, :]).astype(BF16)
    q_spec = pl.BlockSpec((None, n_heads, tp, hd), lambda bb, j, pt: (bb, 0, 0, 0))

    def page_spec(g):
        return pl.BlockSpec((None, None, n_heads, hd, page),
                            lambda bb, j, pt: (layer, pt[bb * n_pages + n_pages - 1 - (j * n_pg + g)], 0, 0, 0))

    const = lambda shape: pl.BlockSpec(shape, lambda bb, j, pt: (0,) * len(shape))
    out = pl.pallas_call(
        functools.partial(_sb_sample_kernel, n_pg, n_heads, tp, page, hd ** -0.5),
        grid_spec=pltpu.PrefetchScalarGridSpec(
            num_scalar_prefetch=1, grid=(b, n_pages // n_pg),
            in_specs=[q_spec] * 3 + [page_spec(g) for g in range(n_pg)] * 2
                     + [const((n_heads * tp, page)), const((page, page))],
            out_specs=q_spec,
            scratch_shapes=[pltpu.VMEM((n_heads * tp, db), BF16),
                            pltpu.VMEM((n_heads, page, hd), F32), pltpu.VMEM((n_heads, page, hd), F32),
                            pltpu.VMEM((n_heads * tp, db), F32), pltpu.VMEM((n_heads * tp, 1), F32)]),
        out_shape=jax.ShapeDtypeStruct((b, n_heads, tp, hd), F32),
        compiler_params=_cparams("arbitrary", "arbitrary"), name="sb_sample",
    )(page_table.reshape(-1), by_head(col(0)), by_head(col(1)), by_head(col(2)),
      *([cache_kt] * n_pg), *([cache_vt] * n_pg), bias_b, tri)
    return out[:, :, :t, :].transpose(0, 2, 1, 3).reshape(b * t, db)


def _merge_kernel(y_ref, r_ref, k_ref, v_ref, og_ref, yb_ref, ga_ref, gb_ref, wa_ref, wb_ref,
                  rk_ref, lg_ref, lb_ref, seg_ref, o_ref, ya_scr):
    @pl.when(pl.program_id(1) == 0)
    def _():
        seg = seg_ref[...]
        y = y_ref[...]
        hd_inv = 1.0 / jnp.sum(seg[0:1, :].astype(F32))
        mu = _head_sum(y, seg) * hd_inv
        yc = y - mu
        var = _head_sum(yc * yc, seg) * hd_inv
        yn = yc * lax.rsqrt(var + GN_EPS) * lg_ref[...] + lb_ref[...]
        bonus = _head_sum(r_ref[...] * k_ref[...] * rk_ref[...], seg)
        ya_scr[...] = ((yn + bonus * v_ref[...]) * og_ref[...]).astype(BF16)
    ma = jnp.dot(ya_scr[...], wa_ref[...], preferred_element_type=F32)
    mb = jnp.dot(yb_ref[...].astype(BF16), wb_ref[...], preferred_element_type=F32)
    o_ref[...] = (_sigmoid(ga_ref[...]) * ma + _sigmoid(gb_ref[...]) * mb).astype(BF16)


def _merge(y, rwkv_r, rwkv_k, rwkv_v, og, y_b, proj_main, gate_off, wa_bf, wb_bf, prm, seg, tm):
    n, da = y.shape
    db = y_b.shape[1]
    d = wa_bf.shape[1]
    tn = _tile(d, 512)
    ga_blk = gate_off // tn
    row = lambda w: pl.BlockSpec((tm, w), lambda i, j: (i, 0))
    vec = lambda w: pl.BlockSpec((1, w), lambda i, j: (0, 0))
    return pl.pallas_call(
        _merge_kernel, grid=(n // tm, d // tn),
        in_specs=[row(da)] * 5 + [row(db),
                  pl.BlockSpec((tm, tn), lambda i, j: (i, ga_blk + j)),
                  pl.BlockSpec((tm, tn), lambda i, j: (i, ga_blk + d // tn + j)),
                  pl.BlockSpec((da, tn), lambda i, j: (0, j)), pl.BlockSpec((db, tn), lambda i, j: (0, j)),
                  vec(da), vec(da), vec(da), pl.BlockSpec((LANES, LANES), lambda i, j: (0, 0))],
        out_specs=pl.BlockSpec((tm, tn), lambda i, j: (i, j)),
        out_shape=jax.ShapeDtypeStruct((n, d), BF16),
        scratch_shapes=[pltpu.VMEM((tm, da), BF16)],
        compiler_params=_cparams("arbitrary", "arbitrary"), name="merge",
    )(y, rwkv_r, rwkv_k, rwkv_v, og, y_b, proj_main, proj_main, wa_bf, wb_bf,
      prm["r_k"], prm["lnx_g"], prm["lnx_b"], seg)


def _wo_ln_kernel(alpha, m_ref, w_ref, x_ref, gm_ref, shf_ref, scf_ref, lg_ref, lb_ref, x1_o, h2_o):
    y = jnp.dot(m_ref[...], w_ref[...], preferred_element_type=F32)
    x1 = _std(alpha * x_ref[...] + gm_ref[...] * y, LN_EPS) * lg_ref[...] + lb_ref[...]
    x1_o[...] = x1
    h2_o[...] = (_std(x1, LN_EPS) * (1.0 + scf_ref[...]) + shf_ref[...]).astype(BF16)


def _wo_ln(merged, x, grp, mod, wo_bf, ln_g, ln_b, alpha, tm):
    n, d = x.shape
    row = pl.BlockSpec((tm, d), lambda i: (i, 0))
    vec = pl.BlockSpec((1, d), lambda i: (0, 0))
    return pl.pallas_call(
        functools.partial(_wo_ln_kernel, alpha), grid=(n // tm,),
        in_specs=[row, pl.BlockSpec((d, d), lambda i: (0, 0)), row,
                  _mod_spec(grp, 2, tm), _mod_spec(grp, 3, tm), _mod_spec(grp, 4, tm), vec, vec],
        out_specs=[row, row],
        out_shape=[jax.ShapeDtypeStruct((n, d), F32), jax.ShapeDtypeStruct((n, d), BF16)],
        compiler_params=_cparams("arbitrary"), name="wo_ln",
    )(merged, wo_bf, x, mod, mod, mod, ln_g, ln_b)


def _peer_q_kernel(ng, hk, h_ref, wq_ref, keys_ref, o_ref):
    qp = jnp.dot(h_ref[...], wq_ref[...], preferred_element_type=F32)
    for g in range(ng):
        o_ref[g] = _dot_nt(keys_ref[g], qp[:, g * hk:(g + 1) * hk].astype(BF16))


def _peer_q(h2, wq_bf, keys_bf, tm):
    n, d = h2.shape
    n_grp, nk, hk = keys_bf.shape
    ng = 4 if n_grp % 4 == 0 else 2
    return pl.pallas_call(
        functools.partial(_peer_q_kernel, ng, hk), grid=(n // tm, n_grp // ng),
        in_specs=[pl.BlockSpec((tm, d), lambda i, j: (i, 0)),
                  pl.BlockSpec((d, ng * hk), lambda i, j: (0, j)),
                  pl.BlockSpec((ng, nk, hk), lambda i, j: (j, 0, 0))],
        out_specs=pl.BlockSpec((ng, nk, tm), lambda i, j: (j, 0, i)),
        out_shape=jax.ShapeDtypeStruct((n_grp, nk, n), F32),
        compiler_params=_cparams("arbitrary", "arbitrary"), name="peer_q",
    )(h2, wq_bf, keys_bf)


def _top_rows(s, k, payloads=(), rid=None):
    if rid is None:
        rid = lax.broadcasted_iota(jnp.int32, s.shape, 0).astype(F32)
    vals, idxs, picked = [], [], [[] for _ in payloads]
    for _ in range(k):
        m = jnp.max(s, axis=0, keepdims=True)
        idx = jnp.min(jnp.where(s == m, rid, RID_NONE), axis=0, keepdims=True)
        hit = rid == idx
        vals.append(m)
        idxs.append(idx)
        for out, pay in zip(picked, payloads):
            out.append(jnp.max(jnp.where(hit, pay, -1.0), axis=0, keepdims=True))
        s = jnp.where(hit, -jnp.inf, s)
    return vals, idxs, picked


def _peer_topk_kernel(hp, s_ref, g_o, i1_o, i2_o):
    for hh in range(hp):
        rows = slice(hh * TOPK, (hh + 1) * TOPK)
        g_o[rows, :], i1_o[rows, :], i2_o[rows, :] = _peer_topk_head(s_ref[2 * hh], s_ref[2 * hh + 1])


def _peer_topk_head(s0, s1):
    tl = s0.shape[1]
    v0, i0, _ = _top_rows(s0, TOPK)
    v1, i1, _ = _top_rows(s1, TOPK)
    v0c, i0c = jnp.concatenate(v0, axis=0), jnp.concatenate(i0, axis=0)
    v1c, i1c = jnp.concatenate(v1, axis=0), jnp.concatenate(i1, axis=0)
    sub = lax.broadcasted_iota(jnp.int32, (SUBLANES, tl), 0)
    cand, rid, c1, c2 = [], [], [], []
    for i in range(TOPK // 2):
        nj = TOPK // (i + 1)
        for j0 in range(0, nj, SUBLANES):
            jj = sub + j0
            cand.append(jnp.where(jj < nj, v0[i] + v1c[j0:j0 + SUBLANES], -jnp.inf))
            rid.append((jj + i * TOPK).astype(F32))
            c1.append(jnp.broadcast_to(i0[i], (SUBLANES, tl)))
            c2.append(i1c[j0:j0 + SUBLANES])
    for i0_ in range(TOPK // 2, TOPK, SUBLANES):
        cand.append(v0c[i0_:i0_ + SUBLANES] + v1[0])
        rid.append(((sub + i0_) * TOPK).astype(F32))
        c1.append(i0c[i0_:i0_ + SUBLANES])
        c2.append(jnp.broadcast_to(i1[0], (SUBLANES, tl)))
    cat = lambda xs: jnp.concatenate(xs, axis=0)
    best, _, (e1, e2) = _top_rows(cat(cand), TOPK, (cat(c1), cat(c2)), rid=cat(rid))
    best = jnp.concatenate(best, axis=0)
    e = jnp.exp(best - best[0:1])
    return e / jnp.sum(e, axis=0, keepdims=True), jnp.concatenate(e1, axis=0), jnp.concatenate(e2, axis=0)


def _peer_topk(scores_t):
    n_grp, nk, n = scores_t.shape
    n_heads = n_grp // 2
    tl = _tile(n, LANES)
    hp = 4 if n_heads % 4 == 0 else 1
    out = jax.ShapeDtypeStruct((n_heads * TOPK, n), F32)
    ospec = pl.BlockSpec((hp * TOPK, tl), lambda i, h: (h, i))
    return pl.pallas_call(
        functools.partial(_peer_topk_kernel, hp), grid=(n // tl, n_heads // hp),
        in_specs=[pl.BlockSpec((2 * hp, nk, tl), lambda i, h: (h, 0, i))],
        out_specs=[ospec] * 3, out_shape=[out] * 3,
        compiler_params=_cparams("arbitrary", "arbitrary"), name="peer_topk",
    )(scores_t)


def _peer_gate_kernel(tg, nk, g_ref, i1_ref, i2_ref, o_ref, g_s, i1_s, i2_s):
    g_s[...] = g_ref[...].T
    i1_s[...] = i1_ref[...].T
    i2_s[...] = i2_ref[...].T
    m = g_s.shape[1]
    sub = lax.broadcasted_iota(jnp.int32, (nk, m), 0).astype(F32)
    steps = BF16_ROWS

    def body(nb, carry):
        rows = pl.ds(pl.multiple_of(nb * steps, steps), steps)
        g8, a8, b8 = g_s[rows, :], i1_s[rows, :], i2_s[rows, :]
        a_t = [jnp.where(a8[i:i + 1, :] == sub, g8[i:i + 1, :], 0.0).astype(BF16) for i in range(steps)]
        b_t = [jnp.where(b8[i:i + 1, :] == sub, 1.0, 0.0).astype(BF16) for i in range(steps)]
        g_tok = [_dot_nt(a_t[i], b_t[i]) for i in range(steps)]
        by_key = jnp.swapaxes(jnp.stack(g_tok, axis=0), 0, 1)
        for c in range(nk):
            o_ref[rows, c * nk:(c + 1) * nk] = by_key[c].astype(BF16)
        return carry

    lax.fori_loop(0, tg // steps, body, 0)


def _peer_gate(gate_t, i1_t, i2_t, nk):
    m, n = gate_t.shape
    tg = _tile(n, LANES)
    ispec = pl.BlockSpec((m, tg), lambda i: (0, i))
    return pl.pallas_call(
        functools.partial(_peer_gate_kernel, tg, nk), grid=(n // tg,),
        in_specs=[ispec] * 3,
        out_specs=pl.BlockSpec((tg, nk * nk), lambda i: (i, 0)),
        out_shape=jax.ShapeDtypeStruct((n, nk * nk), BF16),
        scratch_shapes=[pltpu.VMEM((tg, m), F32)] * 3,
        compiler_params=_cparams("arbitrary"), name="peer_gate",
    )(gate_t, i1_t, i2_t)


def _gelu_tanh(x):
    return 0.5 * x * (1.0 + jnp.tanh(math.sqrt(2.0 / math.pi) * (x + 0.044715 * (x * x * x))))


def _peer_dense_kernel(alpha, h_ref, g_ref, ut_ref, v_ref, x1_ref, gf_ref, lg_ref, lb_ref, o_ref):
    c = pl.program_id(1)

    @pl.when(c == 0)
    def _():
        o_ref[...] = jnp.zeros_like(o_ref)

    act = _gelu_tanh(jnp.dot(h_ref[...], ut_ref[...], preferred_element_type=F32))
    p = (act * g_ref[...].astype(F32)).astype(BF16)
    o_ref[...] += jnp.dot(p, v_ref[...], preferred_element_type=F32)

    @pl.when(c == pl.num_programs(1) - 1)
    def _():
        z = alpha * x1_ref[...] + gf_ref[...] * o_ref[...]
        o_ref[...] = _std(z, LN_EPS) * lg_ref[...] + lb_ref[...]


def _peer_dense(h2, gmat, ut_bf, v_bf, x1, grp, mod, ln_g, ln_b, alpha, tm):
    n, d = x1.shape
    e = v_bf.shape[0]
    te = _tile(e, 1024)
    row = pl.BlockSpec((tm, d), lambda i, c: (i, 0))
    vec = pl.BlockSpec((1, d), lambda i, c: (0, 0))
    return pl.pallas_call(
        functools.partial(_peer_dense_kernel, alpha), grid=(n // tm, e // te),
        in_specs=[row, pl.BlockSpec((tm, te), lambda i, c: (i, c)),
                  pl.BlockSpec((d, te), lambda i, c: (0, c)), pl.BlockSpec((te, d), lambda i, c: (c, 0)),
                  row, _mod_spec(grp, 5, tm), vec, vec],
        out_specs=row, out_shape=jax.ShapeDtypeStruct((n, d), F32),
        compiler_params=_cparams("arbitrary", "arbitrary"), name="peer_dense",
    )(h2, gmat, ut_bf, v_bf, x1, mod, ln_g, ln_b)


def _run_trunk(x, grp, mods, shift0, wkv0, cache_k, cache_v, page_table, w, dims):
    d, da, db, hd, lp, depth = dims["d"], dims["da"], dims["db"], dims["hd"], dims["lp"], dims["depth"]
    b, t = grp["b"], grp["t"]
    n = b * t
    tm = grp["tm"]
    alpha = (2 * depth) ** 0.25
    n_raw = 3 * da + dims["n_lora"]
    seg, diag = w["seg"], w["diag"]
    k_rows, v_rows, wkv_rows, shift_rows = [], [], [], []
    for l in range(depth):
        mod = mods[l]
        lw = w["layers"][l]
        proj_main = _proj(x, grp, mod, lw["w_main"], _tile(da, 1024))
        proj_lora = _proj(x, grp, mod, lw["w_lora"], lp)
        if grp["per_row"]:
            prev_main = jnp.repeat(shift0[l][:, :3 * da], t, axis=0)
            prev_lora = jnp.repeat(jnp.pad(shift0[l][:, 3 * da:], ((0, 0), (0, lp - dims["n_lora"]))), t, axis=0)
        else:
            prev_main = shift0[l][:, None, :3 * da]
            prev_lora = jnp.pad(shift0[l][:, None, 3 * da:], ((0, 0), (0, 0), (0, lp - dims["n_lora"])))
        r_, w_, k_, v_, a_, b_, og = _rwkv_prep(proj_main, proj_lora, grp, prev_main, prev_lora, lw, seg)
        if t % (LANES // (LANES // hd)) == 0:
            y, s_fin = _wkv_chunked((r_, w_, k_, v_, a_, b_), wkv0[l], grp, hd)
        else:
            y, s_fin = _scan((r_, w_, k_, v_, a_, b_), wkv0[l], grp, seg, diag)
        if cache_k is None:
            y_b = _sb_prompt(proj_main, grp, lw["sb_bias"], db, hd, 3 * da)
        else:
            y_b = _sb_sample(proj_main, grp, cache_k, cache_v, l, page_table, lw["sb_bias"], db, hd, 3 * da)
        merged = _merge(y, r_, k_, v_, og, y_b, proj_main, 3 * da + 3 * db, lw["w_br_a"], lw["w_br_b"], lw, seg,
                        grp["tm"])
        x1, h2 = _wo_ln(merged, x, grp, mod, lw["wo"], lw["ln1_g"], lw["ln1_b"], alpha, grp["tm_prep"])
        scores_t = _peer_q(h2, lw["wq"], lw["keys"], tm)
        gate_t, i1_t, i2_t = _peer_topk(scores_t)
        nk = lw["keys"].shape[1]
        gmat = _peer_gate(gate_t, i1_t, i2_t, nk)
        x = _peer_dense(h2, gmat, lw["ut"], lw["v"], x1, grp, mod, lw["ln2_g"], lw["ln2_b"], alpha, tm)
        p3 = proj_main.reshape(b, t, -1)
        k_rows.append(p3[:, :, 3 * da + db:3 * da + 2 * db].reshape(b, t, db // hd, hd))
        v_rows.append(p3[:, :, 3 * da + 2 * db:3 * da + 3 * db].reshape(b, t, db // hd, hd))
        wkv_rows.append(s_fin)
        shift_rows.append(jnp.concatenate(
            [p3[:, -1, :3 * da], proj_lora.reshape(b, t, lp)[:, -1, :dims["n_lora"]]], axis=-1))
        assert shift_rows[-1].shape[-1] == n_raw
    return (x.reshape(b, t, d), jnp.stack(k_rows), jnp.stack(v_rows), jnp.stack(wkv_rows), jnp.stack(shift_rows))


def kernel(x_prompt, x_sample, cache_k, cache_v, state_wkv, state_shift, page_table, c_prompt, c_sample,
           w_ada, b_ada, w_in, mu_shift, decay_bias, decay_up, aaa_bias, aaa_up, og_up, k_k, k_a, r_k,
           lnx_g, lnx_b, sb_bias, w_branch_a, w_branch_b, w_o, ln1_g, ln1_b, peer_wq, peer_keys, peer_u, peer_v,
           ln2_g, ln2_b):
    depth, d, _ = w_ada.shape
    hd = cache_k.shape[-1]
    da = decay_up.shape[2]
    db = w_branch_b.shape[1]
    wl, al, gl = decay_up.shape[1], aaa_up.shape[1], og_up.shape[1]
    n_lora = wl + al + gl
    lp = -(-n_lora // LANES) * LANES
    assert da % LANES == 0 and db % LANES == 0 and LANES % hd == 0
    dims = dict(d=d, da=da, db=db, hd=hd, lp=lp, depth=depth, n_lora=n_lora)

    seg = (jnp.arange(LANES)[:, None] // hd == jnp.arange(LANES)[None, :] // hd).astype(BF16)
    diag = (jnp.arange(hd)[:, None] == jnp.arange(LANES)[None, :] % hd).astype(F32)
    layers = []
    for l in range(depth):
        rw = 3 * da + n_lora
        pad_rows = lambda m, off: jnp.pad(m, ((off, lp - off - m.shape[0]), (0, 0))).astype(BF16)
        n_grp = peer_keys.shape[1] * peer_keys.shape[2]
        layers.append(dict(
            da=da, lp=lp,
            w_main=jnp.concatenate([w_in[l][:, :3 * da], w_in[l][:, rw:]], axis=1).astype(BF16),
            w_lora=jnp.pad(w_in[l][:, 3 * da:rw], ((0, 0), (0, lp - n_lora))).astype(BF16),
            mu_main=mu_shift[l][None, :3 * da],
            mu_lora=jnp.pad(mu_shift[l][None, 3 * da:], ((0, 0), (0, lp - n_lora))),
            decay_bias=decay_bias[l][None], aaa_bias=aaa_bias[l][None],
            wd=pad_rows(decay_up[l], 0), wa=pad_rows(aaa_up[l], wl), wg=pad_rows(og_up[l], wl + al),
            w_br_a=w_branch_a[l].astype(BF16), w_br_b=w_branch_b[l].astype(BF16),
            k_k=k_k[l][None], k_a=k_a[l][None], r_k=r_k[l].reshape(1, da),
            lnx_g=lnx_g[l][None], lnx_b=lnx_b[l][None], sb_bias=sb_bias[l],
            wo=w_o[l].astype(BF16), ln1_g=ln1_g[l][None], ln1_b=ln1_b[l][None],
            wq=peer_wq[l].astype(BF16), keys=peer_keys[l].reshape(n_grp, *peer_keys.shape[3:]).astype(BF16),
            ut=peer_u[l].T.astype(BF16), v=peer_v[l].astype(BF16),
            ln2_g=ln2_g[l][None], ln2_b=ln2_b[l][None]))
    w = dict(seg=seg, diag=diag, layers=layers)

    bp, tp_, _ = x_prompt.shape
    bs, ts, _ = x_sample.shape
    mod_all = _ada(jnp.concatenate([c_prompt, c_sample], axis=0), w_ada, b_ada)

    def group(b, t):
        n = b * t
        tm = _tile(n, 512) if t < 512 else _tile(t, 512)
        tm_prep = _tile(n, 256) if t < 256 else _tile(t, 256)
        tm_big = _tile(n, 1024) if t < tm else _tile(t, 1024)
        return dict(b=b, t=t, d=d, tm=tm, tm_prep=tm_prep, tm_big=tm_big, per_row=t < tm)

    gp, gs = group(bp, tp_), group(bs, ts)
    assert gp["per_row"] == (gp["t"] < gp["tm_prep"]) and gs["per_row"] == (gs["t"] < gs["tm_prep"])

    def mods_for(grp, lo, hi):
        out = []
        for l in range(depth):
            m = mod_all[l, lo:hi]
            out.append(jnp.repeat(m, grp["t"], axis=0) if grp["per_row"] else m.reshape(hi - lo, 6, 1, d))
        return out

    shift0_p = jnp.zeros((depth, bp, 3 * da + n_lora), F32)
    wkv0_p = jnp.zeros((depth, bp, da // hd, hd, hd), F32)
    y_p, k_p, v_p, wkv_p, shift_p = _run_trunk(
        x_prompt.reshape(bp * tp_, d), gp, mods_for(gp, 0, bp), shift0_p, wkv0_p, None, None, None, w, dims)
    y_s, k_s, v_s, wkv_s, shift_s = _run_trunk(
        x_sample.reshape(bs * ts, d), gs, mods_for(gs, bp, bp + bs), state_shift, state_wkv,
        cache_k.transpose(0, 1, 3, 4, 2), cache_v.transpose(0, 1, 3, 4, 2), page_table, w, dims)
    return (y_p, y_s, k_p, v_p, wkv_p, shift_p, k_s, v_s, wkv_s, shift_s)
```

```python
import functools
import math

import jax
import jax.numpy as jnp
from jax import lax
from jax.experimental import pallas as pl
from jax.experimental.pallas import tpu as pltpu

F32 = jnp.float32
BF16 = jnp.bfloat16
LN_EPS = 1e-5
GN_EPS = 64e-5
KK_EPS = 1e-12
TOPK = 16
LANES = 128
SUBLANES = 8
BF16_ROWS = 16
RID_NONE = 1e9
SB_UNROLL = 2
VMEM_LIMIT = 56 * 1024 * 1024


def _cparams(*sem):
    return pltpu.CompilerParams(dimension_semantics=sem, vmem_limit_bytes=VMEM_LIMIT)


def _tile(n, pref):
    if n <= pref:
        return n
    t = (pref // LANES) * LANES
    while t > LANES and n % t:
        t -= LANES
    assert n % t == 0, (n, pref)
    return t


def _std(x, eps):
    mu = jnp.mean(x, axis=-1, keepdims=True)
    xc = x - mu
    var = jnp.mean(xc * xc, axis=-1, keepdims=True)
    return xc * lax.rsqrt(var + eps)


def _sigmoid(x):
    return 1.0 / (1.0 + jnp.exp(-x))


def _softplus(x):
    return jnp.maximum(x, 0.0) + jnp.log(1.0 + jnp.exp(-jnp.abs(x)))


def _split_dot(x, m, *, left=False):
    hi = x.astype(BF16)
    lo = (x - hi.astype(F32)).astype(BF16)
    if left:
        return (jnp.dot(m, hi, preferred_element_type=F32) + jnp.dot(m, lo, preferred_element_type=F32))
    return (jnp.dot(hi, m, preferred_element_type=F32) + jnp.dot(lo, m, preferred_element_type=F32))


def _head_sum(x, seg):
    outs = [_split_dot(x[:, c * LANES:(c + 1) * LANES], seg) for c in range(x.shape[1] // LANES)]
    return outs[0] if len(outs) == 1 else jnp.concatenate(outs, axis=1)


def _dot_nt(a, b):
    return lax.dot_general(a, b, (((1,), (1,)), ((), ())), preferred_element_type=F32)


def _dot_tn(a, b):
    return lax.dot_general(a, b, (((0,), (0,)), ((), ())), preferred_element_type=F32)


def _ada_kernel(c_ref, w_ref, b_ref, o_ref):
    c = c_ref[...]
    s = (c * _sigmoid(c)).astype(BF16)
    o_ref[...] = jnp.dot(s, w_ref[...].astype(BF16), preferred_element_type=F32) + b_ref[...]


def _ada(c_all, w_ada, b_ada):
    depth, d, d6 = w_ada.shape
    nb = c_all.shape[0]
    tn = _tile(d6, 1024)
    return pl.pallas_call(
        _ada_kernel, grid=(depth, d6 // tn),
        in_specs=[pl.BlockSpec((nb, d), lambda l, j: (0, 0)),
                  pl.BlockSpec((None, d, tn), lambda l, j: (l, 0, j)),
                  pl.BlockSpec((None, 1, tn), lambda l, j: (l, 0, j))],
        out_specs=pl.BlockSpec((None, nb, tn), lambda l, j: (l, 0, j)),
        out_shape=jax.ShapeDtypeStruct((depth, nb, d6), F32),
        compiler_params=_cparams("arbitrary", "arbitrary"), name="ada",
    )(c_all, w_ada, b_ada.reshape(depth, 1, d6))


def _mod_spec(grp, which, tm):
    d = grp["d"]
    if grp["per_row"]:
        return pl.BlockSpec((tm, d), lambda i, *_: (i, which))
    t = grp["t"]
    return pl.BlockSpec((None, None, 1, d), lambda i, *_: ((i * tm) // t, which, 0, 0))


def _proj_kernel(x_ref, sh_ref, sc_ref, w_ref, o_ref, h_scr):
    @pl.when(pl.program_id(1) == 0)
    def _():
        tm = x_ref.shape[0]
        rc = min(tm, 256)

        def chunk(c, carry):
            rows = pl.ds(pl.multiple_of(c * rc, rc), rc)
            vec = lambda ref: ref[...] if ref.shape[0] == 1 else ref[rows, :]
            h = _std(x_ref[rows, :], LN_EPS) * (1.0 + vec(sc_ref)) + vec(sh_ref)
            h_scr[rows, :] = h.astype(BF16)
            return carry

        lax.fori_loop(0, tm // rc, chunk, 0)
    o_ref[...] = jnp.dot(h_scr[...], w_ref[...], preferred_element_type=F32)


def _proj(x, grp, mod, w_bf, tn):
    n, d = x.shape
    p = w_bf.shape[1]
    tm = grp["tm_big"]
    return pl.pallas_call(
        _proj_kernel, grid=(n // tm, p // tn),
        in_specs=[pl.BlockSpec((tm, d), lambda i, j: (i, 0)),
                  _mod_spec(grp, 0, tm), _mod_spec(grp, 1, tm),
                  pl.BlockSpec((d, tn), lambda i, j: (0, j))],
        out_specs=pl.BlockSpec((tm, tn), lambda i, j: (i, j)),
        out_shape=jax.ShapeDtypeStruct((n, p), F32),
        scratch_shapes=[pltpu.VMEM((tm, d), BF16)],
        compiler_params=_cparams("arbitrary", "arbitrary"), name="proj",
    )(x, mod, mod, w_bf)


def _rwkv_prep_kernel(t_seq, tm, da, per_row, pm_ref, pl_ref, prevm_ref, prevl_ref, mum_ref, mul_ref,
                      dbias_ref, abias_ref, wd_ref, wa_ref, wg_ref, kk_ref, ka_ref, seg_ref,
                      r_o, w_o, k_o, v_o, a_o, b_o, og_o, carm, carl):
    i = pl.program_id(0)
    row = lax.broadcasted_iota(jnp.int32, (tm, 1), 0)

    def shifted(p, prev_ref, car):
        rolled = pltpu.roll(p, 1, 0)
        if per_row:
            return jnp.where(lax.rem(row, t_seq) == 0, prev_ref[...], rolled)
        first = jnp.where(lax.rem(i, t_seq // tm) == 0, prev_ref[...], car[...])
        car[...] = p[tm - 1:tm, :]
        return jnp.where(row == 0, first, rolled)

    p = pm_ref[...]
    pm = p + (shifted(p, prevm_ref, carm) - p) * mum_ref[...]
    q = pl_ref[...]
    lo = q + (shifted(q, prevl_ref, carl) - q) * mul_ref[...]
    r = pm[:, :da]
    k = pm[:, da:2 * da]
    v = pm[:, 2 * da:3 * da]
    w_pre = dbias_ref[...] + jnp.dot(jnp.tanh(lo).astype(BF16), wd_ref[...], preferred_element_type=F32)
    decay = jnp.exp(-math.exp(-0.5) * _sigmoid(w_pre))
    a = _sigmoid(abias_ref[...] + jnp.dot(lo.astype(BF16), wa_ref[...], preferred_element_type=F32))
    og = jnp.dot(_sigmoid(lo).astype(BF16), wg_ref[...], preferred_element_type=F32)
    kk = k * kk_ref[...]
    kk = kk * lax.rsqrt(_head_sum(kk * kk, seg_ref[...]) + KK_EPS)
    r_o[...] = r
    w_o[...] = decay
    k_o[...] = k * (1.0 + (a - 1.0) * ka_ref[...])
    v_o[...] = v
    a_o[...] = -kk
    b_o[...] = kk * a
    og_o[...] = og


def _rwkv_prep(proj_main, proj_lora, grp, prev_main, prev_lora, prm, seg):
    n = proj_main.shape[0]
    da, lp = prm["da"], prm["lp"]
    tm = grp["tm_prep"]
    per_row = grp["t"] < tm
    row_spec = lambda w: pl.BlockSpec((tm, w), lambda i: (i, 0))
    vec_spec = lambda w: pl.BlockSpec((1, w), lambda i: (0, 0))
    if per_row:
        prev_specs = [row_spec(3 * da), row_spec(lp)]
    else:
        t = grp["t"]
        prev_specs = [pl.BlockSpec((None, 1, 3 * da), lambda i: ((i * tm) // t, 0, 0)),
                      pl.BlockSpec((None, 1, lp), lambda i: ((i * tm) // t, 0, 0))]
    out = jax.ShapeDtypeStruct((n, da), F32)
    return pl.pallas_call(
        functools.partial(_rwkv_prep_kernel, grp["t"], tm, da, per_row), grid=(n // tm,),
        in_specs=[row_spec(3 * da), row_spec(lp)] + prev_specs + [
            vec_spec(3 * da), vec_spec(lp), vec_spec(da), vec_spec(da),
            pl.BlockSpec((lp, da), lambda i: (0, 0)), pl.BlockSpec((lp, da), lambda i: (0, 0)),
            pl.BlockSpec((lp, da), lambda i: (0, 0)), vec_spec(da), vec_spec(da),
            pl.BlockSpec((LANES, LANES), lambda i: (0, 0))],
        out_specs=[row_spec(da)] * 7, out_shape=[out] * 7,
        scratch_shapes=[pltpu.VMEM((1, 3 * da), F32), pltpu.VMEM((1, lp), F32)],
        compiler_params=_cparams("arbitrary"), name="rwkv_prep",
    )(proj_main, proj_lora, prev_main, prev_lora, prm["mu_main"], prm["mu_lora"], prm["decay_bias"],
      prm["aaa_bias"], prm["wd"], prm["wa"], prm["wg"], prm["k_k"], prm["k_a"], seg)


def _scan_kernel(tt, n_pairs, bb, r_ref, w_ref, k_ref, v_ref, a_ref, b_ref, s0_ref, seg_ref, dm_ref,
                 y_ref, sT_ref, s_scr):
    tc = pl.program_id(1)

    @pl.when(tc == 0)
    def _():
        s_scr[...] = s0_ref[...]

    seg = seg_ref[...]
    diag = dm_ref[...]
    steps = min(tt, SUBLANES)
    chains = [(s, p) for s in range(bb) for p in range(n_pairs)]
    lanes = lambda p: slice(p * LANES, (p + 1) * LANES)

    def block(tb, carry):
        rows = pl.ds(pl.multiple_of(tb * steps, steps), steps)
        r8, w8, k8, v8, a8, b8 = [[ref[s, rows, :] for s in range(bb)]
                                  for ref in (r_ref, w_ref, k_ref, v_ref, a_ref, b_ref)]
        ys = [[] for _ in range(bb)]
        for i in range(steps):
            row = lambda x, c: x[c[0]][i:i + 1, lanes(c[1])]
            st = [s_scr[s, :, lanes(p)] for s, p in chains]
            sa = [_split_dot(st[n] * row(a8, c), seg) for n, c in enumerate(chains)]
            vcol = [_split_dot(diag * row(v8, c), seg) for c in chains]
            st = [st[n] * row(w8, c) + sa[n] * row(b8, c) + vcol[n] * row(k8, c) for n, c in enumerate(chains)]
            for n, (s, p) in enumerate(chains):
                s_scr[s, :, lanes(p)] = st[n]
            yb = [_split_dot(st[n] * row(r8, c), seg) for n, c in enumerate(chains)]
            yv = [jnp.sum(x * diag, axis=0, keepdims=True) for x in yb]
            for s in range(bb):
                part = yv[s * n_pairs:(s + 1) * n_pairs]
                ys[s].append(part[0] if n_pairs == 1 else jnp.concatenate(part, axis=1))
        for s in range(bb):
            y_ref[s, rows, :] = jnp.concatenate(ys[s], axis=0)
        return carry

    lax.fori_loop(0, tt // steps, block, 0)

    @pl.when(tc == pl.num_programs(1) - 1)
    def _():
        sT_ref[...] = s_scr[...]


def _wkv_chunk_kernel(n_sub, n_pairs, c, hd, r_ref, w_ref, k_ref, v_ref, a_ref, b_ref, h0_ref, y_ref, hT_ref, h_scr):
    tc = pl.program_id(1)
    nh = LANES // hd
    rows_st = nh * c

    @pl.when(tc == 0)
    def _():
        h_scr[...] = h0_ref[...]

    lane_head = lax.broadcasted_iota(jnp.int32, (1, LANES), 1) // hd
    ri = lax.broadcasted_iota(jnp.int32, (rows_st, rows_st), 0)
    ci = lax.broadcasted_iota(jnp.int32, (rows_st, rows_st), 1)
    same_head = (ri // c) == (ci // c)
    strict = same_head & (ci < ri)
    incl = same_head & (ci <= ri)
    ti = lax.broadcasted_iota(jnp.int32, (c, c), 0)
    tj = lax.broadcasted_iota(jnp.int32, (c, c), 1)
    tri_incl = jnp.where(tj <= ti, 1.0, 0.0).astype(BF16)
    ones_3c = jnp.ones((3 * c, LANES), BF16)

    def stack(x):
        return jnp.concatenate([jnp.where(lane_head == h, x, 0.0) for h in range(nh)], axis=0).astype(BF16)

    def dot(a, b):
        return jnp.dot(a, b, preferred_element_type=F32)

    def sub_chunk(sc, carry):
        rows = pl.ds(pl.multiple_of(sc * c, c), c)
        pairs = range(n_pairs)
        sls = [slice(p * LANES, (p + 1) * LANES) for p in pairs]
        ld = lambda ref: [ref[rows, sl] for sl in sls]
        r, w, k, v, a, b = ld(r_ref), ld(w_ref), ld(k_ref), ld(v_ref), ld(a_ref), ld(b_ref)
        lw = [jnp.log(x) for x in w]
        cum = [_split_dot(x, tri_incl, left=True) for x in lw]
        p_inv = [jnp.exp(-x) for x in cum]
        a_st = [stack(a[p] * jnp.exp(cum[p] - lw[p])) for p in pairs]
        r_st = [stack(r[p] * jnp.exp(cum[p])) for p in pairs]
        b_st = [stack(b[p] * p_inv[p]) for p in pairs]
        k_st = [stack(k[p] * p_inv[p]) for p in pairs]
        v_st = [stack(x) for x in v]
        h = [h_scr[p] for p in pairs]
        hb = [x.astype(BF16) for x in h]
        cat0 = lambda *xs: jnp.concatenate(xs, axis=0)
        cat1 = lambda *xs: jnp.concatenate(xs, axis=1)
        cross = [_dot_nt(cat0(a_st[p], r_st[p]), cat0(b_st[p], k_st[p])) for p in pairs]
        npow = [jnp.where(strict, x[:rows_st, :rows_st], 0.0).astype(BF16) for x in cross]
        n_ak = [jnp.where(strict, x[:rows_st, rows_st:], 0.0).astype(BF16) for x in cross]
        m_rb = [jnp.where(incl, x[rows_st:, :rows_st], 0.0).astype(BF16) for x in cross]
        m_rk = [jnp.where(incl, x[rows_st:, rows_st:], 0.0).astype(BF16) for x in cross]
        u = [dot(cat1(a_st[p], n_ak[p]), cat0(hb[p], v_st[p])) for p in pairs]
        span = 1
        while span < c:
            span *= 2
            if span < c:
                both = [dot(npow[p], cat1(u[p].astype(BF16), npow[p])) for p in pairs]
                u = [u[p] + both[p][:, :LANES] for p in pairs]
                npow = [x[:, LANES:].astype(BF16) for x in both]
            else:
                u = [u[p] + dot(npow[p], u[p].astype(BF16)) for p in pairs]
        ub = [x.astype(BF16) for x in u]
        for p in pairs:
            y_st = dot(cat1(r_st[p], m_rb[p], m_rk[p]), cat0(hb[p], ub[p], v_st[p]))
            y = y_st[0:c]
            for hh in range(1, nh):
                y = y + y_st[hh * c:(hh + 1) * c]
            y_ref[rows, sls[p]] = y
        for p in pairs:
            l1 = lw[p].astype(BF16)
            r1 = lw[p] - l1.astype(F32)
            l2 = r1.astype(BF16)
            l3 = (r1 - l2.astype(F32)).astype(BF16)
            tot_col = _dot_tn(cat0(l1, l2, l3), ones_3c)
            p_end = jnp.exp(cum[p][c - 1:c, :] - cum[p])
            h_scr[p] = h[p] * jnp.exp(tot_col) + _dot_tn(cat0(stack(b[p] * p_end), stack(k[p] * p_end)),
                                                         cat0(ub[p], v_st[p]))
        return carry

    lax.fori_loop(0, n_sub, sub_chunk, 0)

    @pl.when(tc == pl.num_programs(1) - 1)
    def _():
        hT_ref[...] = h_scr[...]


def _wkv_chunked(rwkv, wkv0, grp, hd):
    b, t = grp["b"], grp["t"]
    da = rwkv[0].shape[1]
    nh = LANES // hd
    n_pairs = da // LANES
    c = LANES // nh
    tt = min(t, 256)
    eye = jnp.eye(nh, dtype=F32)
    h0 = wkv0.transpose(0, 1, 3, 2).reshape(b, n_pairs, nh, hd, 1, hd) * eye[None, None, :, None, :, None]
    h0 = h0.reshape(b, n_pairs, LANES, LANES)
    seq_spec = pl.BlockSpec((None, tt, da), lambda i, j: (i, j, 0))
    st_spec = pl.BlockSpec((None, n_pairs, LANES, LANES), lambda i, j: (i, 0, 0, 0))
    y, h_fin = pl.pallas_call(
        functools.partial(_wkv_chunk_kernel, tt // c, n_pairs, c, hd), grid=(b, t // tt),
        in_specs=[seq_spec] * 6 + [st_spec], out_specs=[seq_spec, st_spec],
        out_shape=[jax.ShapeDtypeStruct((b, t, da), F32), jax.ShapeDtypeStruct((b, n_pairs, LANES, LANES), F32)],
        scratch_shapes=[pltpu.VMEM((n_pairs, LANES, LANES), F32)],
        compiler_params=_cparams("arbitrary", "arbitrary"), name="wkv_chunked",
    )(*[x.reshape(b, t, da) for x in rwkv], h0)
    h6 = h_fin.reshape(b, n_pairs, nh, hd, nh, hd)
    s_fin = jnp.stack([h6[:, :, h, :, h, :] for h in range(nh)], axis=2)
    return y.reshape(b * t, da), s_fin.reshape(b, n_pairs * nh, hd, hd).transpose(0, 1, 3, 2)


def _scan(rwkv, wkv0, grp, seg, diag):
    b, t = grp["b"], grp["t"]
    da = rwkv[0].shape[1]
    hd = wkv0.shape[-1]
    s0 = wkv0.transpose(0, 2, 1, 3).reshape(b, hd, da)
    tt = min(t, 256)
    bb = 4 if (t <= SUBLANES and b % 4 == 0) else 1
    seq_spec = pl.BlockSpec((bb, tt, da), lambda i, j: (i, j, 0))
    st_spec = pl.BlockSpec((bb, hd, da), lambda i, j: (i, 0, 0))
    y, s_fin = pl.pallas_call(
        functools.partial(_scan_kernel, tt, da // LANES, bb), grid=(b // bb, t // tt),
        in_specs=[seq_spec] * 6 + [st_spec, pl.BlockSpec((LANES, LANES), lambda i, j: (0, 0)),
                                   pl.BlockSpec((hd, LANES), lambda i, j: (0, 0))],
        out_specs=[seq_spec, st_spec],
        out_shape=[jax.ShapeDtypeStruct((b, t, da), F32), jax.ShapeDtypeStruct((b, hd, da), F32)],
        scratch_shapes=[pltpu.VMEM((bb, hd, da), F32)],
        compiler_params=_cparams("arbitrary", "arbitrary"), name="wkv_scan",
    )(*[x.reshape(b, t, da) for x in rwkv], s0, seg, diag)
    return y.reshape(b * t, da), s_fin.reshape(b, hd, da // hd, hd).transpose(0, 2, 1, 3)


def _sb_prompt_kernel(tq, hd, scale, bias_ref, q_ref, k_ref, v_ref, tri_ref, ones_ref, o_ref):
    p, i = pl.program_id(1), pl.program_id(2)
    nh = LANES // hd
    lane = lax.broadcasted_iota(jnp.int32, (1, LANES), 1)
    hmasks = [(lane >= hh * hd) & (lane < (hh + 1) * hd) for hh in range(nh)]
    q = q_ref[...] * scale
    qhs = [jnp.where(hm, q, 0.0).astype(BF16) for hm in hmasks]
    biases = [bias_ref[p * nh + hh] for hh in range(nh)]
    tri = tri_ref[...]

    def key_blocks(js, carry, vis):
        acc, runs = carry
        rows = [pl.ds(pl.multiple_of(j * tq, tq), tq) for j in js]
        kbs = [k_ref[r, :].astype(BF16) for r in rows]
        vs = [v_ref[r, :] for r in rows]
        chains = [(g, hh) for g in range(len(js)) for hh in range(nh)]
        nch = len(chains)
        runs = list(runs)
        zs, es, cums, tots, wts = [None] * nch, [None] * nch, [None] * nch, [None] * nch, [None] * nch

        def scores(n):
            g, hh = chains[n]
            zs[n] = _dot_nt(qhs[hh], kbs[g]) + biases[hh]

        def front(n):
            sp = _softplus(zs[n])
            es[n] = zs[n] - sp
            x = (sp if vis is None else jnp.where(vis, sp, 0.0)).astype(BF16)
            cums[n] = jnp.dot(x, tri, preferred_element_type=F32)
            tots[n] = jnp.dot(x, ones_ref[...], preferred_element_type=F32)[:, 0:1]

        def back(n):
            hh = chains[n][1]
            w = jnp.exp(es[n] - cums[n] - runs[hh])
            wts[n] = (w if vis is None else jnp.where(vis, w, 0.0)).astype(BF16)
            runs[hh] = runs[hh] + tots[n]

        for n in range(min(2, nch)):
            scores(n)
        for n in range(nch):
            front(n)
            if n + 2 < nch:
                scores(n + 2)
            if n >= 1:
                back(n - 1)
        back(nch - 1)
        vhs = [jnp.where(hmasks[hh], vs[g], 0.0).astype(BF16) for g, hh in chains]
        acc = acc + jnp.dot(jnp.concatenate(wts, axis=1), jnp.concatenate(vhs, axis=0),
                            preferred_element_type=F32)
        return acc, tuple(runs)

    qrow = lax.broadcasted_iota(jnp.int32, (tq, 1), 0)
    kcol = lax.broadcasted_iota(jnp.int32, (1, tq), 1)
    carry = (jnp.zeros((tq, LANES), F32), tuple(jnp.zeros((tq, 1), F32) for _ in range(nh)))
    carry = key_blocks([i], carry, kcol < qrow)
    carry = lax.fori_loop(
        0, i // SB_UNROLL,
        lambda s, c: key_blocks([i - 1 - SB_UNROLL * s - g for g in range(SB_UNROLL)], c, None), carry)
    rem = lax.rem(i, SB_UNROLL)
    carry = lax.fori_loop(0, rem, lambda s, c: key_blocks([rem - 1 - s], c, None), carry)
    o_ref[...] = carry[0]


def _sb_prompt(proj_main, grp, sb_bias, db, hd, q_off):
    b, t = grp["b"], grp["t"]
    n = proj_main.shape[0]
    tq = min(t, 256)
    nq = t // tq
    cb = q_off // LANES
    tri = (jnp.arange(tq)[:, None] > jnp.arange(tq)[None, :]).astype(BF16)
    return pl.pallas_call(
        functools.partial(_sb_prompt_kernel, tq, hd, hd ** -0.5),
        grid=(b, db // LANES, nq),
        in_specs=[pl.BlockSpec(memory_space=pltpu.SMEM),
                  pl.BlockSpec((tq, LANES), lambda bb, p, i: (bb * nq + i, cb + p)),
                  pl.BlockSpec((t, LANES), lambda bb, p, i: (bb, cb + db // LANES + p)),
                  pl.BlockSpec((t, LANES), lambda bb, p, i: (bb, cb + 2 * (db // LANES) + p)),
                  pl.BlockSpec((tq, tq), lambda bb, p, i: (0, 0)),
                  pl.BlockSpec((tq, LANES), lambda bb, p, i: (0, 0))],
        out_specs=pl.BlockSpec((tq, LANES), lambda bb, p, i: (bb * nq + i, p)),
        out_shape=jax.ShapeDtypeStruct((n, db), F32),
        compiler_params=_cparams("arbitrary", "arbitrary", "arbitrary"), name="sb_prompt",
    )(sb_bias, proj_main, proj_main, proj_main, tri, jnp.ones((tq, LANES), BF16))


def _sb_sample_kernel(n_pg, n_heads, tp, page, scale, pt_ref, q_ref, kn_ref, vn_ref, *rest):
    kc_refs, vc_refs = rest[:n_pg], rest[n_pg:2 * n_pg]
    bias_ref, tri_ref, o_ref, qbd_scr, kn_scr, vn_scr, acc_scr, run_scr = rest[2 * n_pg:]
    j = pl.program_id(1)
    rows = n_heads * tp
    tri = tri_ref[...]
    hd = q_ref.shape[2]
    flat = lambda x: x.reshape(n_heads * hd, page).astype(BF16)

    @pl.when(j == 0)
    def _():
        qs = q_ref[...] * scale
        qbd_scr[...] = jnp.zeros_like(qbd_scr)
        for h in range(n_heads):
            qbd_scr[h * tp:(h + 1) * tp, h * hd:(h + 1) * hd] = qs[h].astype(BF16)
        kn_scr[...] = jnp.zeros_like(kn_scr)
        vn_scr[...] = jnp.zeros_like(vn_scr)
        kn_scr[:, 0:tp, :] = kn_ref[...]
        vn_scr[:, 0:tp, :] = vn_ref[...]
        z = lax.dot_general(qs.astype(BF16), kn_scr[...].astype(BF16), (((2,), (2,)), ((0,), (0,))),
                            preferred_element_type=F32).reshape(rows, page) + bias_ref[...]
        vis = (lax.broadcasted_iota(jnp.int32, (rows, page), 1)
               < lax.rem(lax.broadcasted_iota(jnp.int32, (rows, page), 0), tp))
        sp = _softplus(z)
        nlk = jnp.where(vis, sp, 0.0)
        cum = jnp.dot(nlk.astype(BF16), tri, preferred_element_type=F32)
        w = jnp.where(vis, jnp.exp(z - sp - cum), 0.0).astype(BF16).reshape(n_heads, tp, page)
        out = lax.dot_general(w, vn_scr[...].astype(BF16), (((2,), (1,)), ((0,), (0,))),
                              preferred_element_type=F32)
        acc_scr[...] = jnp.zeros_like(acc_scr)
        for h in range(n_heads):
            acc_scr[h * tp:(h + 1) * tp, h * hd:(h + 1) * hd] = out[h]
        run_scr[...] = jnp.sum(nlk, axis=1, keepdims=True)

    z_all = jnp.dot(qbd_scr[...], jnp.concatenate([flat(ref[...]) for ref in kc_refs], axis=1),
                    preferred_element_type=F32)
    zs = [z_all[:, g * page:(g + 1) * page] + bias_ref[...] for g in range(n_pg)]
    sps = [_softplus(z) for z in zs]
    cums = [jnp.dot(sp.astype(BF16), tri, preferred_element_type=F32) for sp in sps]
    run = run_scr[...]
    wts = []
    for g in range(n_pg):
        wts.append(jnp.exp(zs[g] - sps[g] - cums[g] - run).astype(BF16))
        run = run + jnp.sum(sps[g], axis=1, keepdims=True)
    run_scr[...] = run
    acc_scr[...] += _dot_nt(jnp.concatenate(wts, axis=1),
                            jnp.concatenate([flat(ref[...]) for ref in vc_refs], axis=1))

    @pl.when(j == pl.num_programs(1) - 1)
    def _():
        for h in range(n_heads):
            o_ref[h] = acc_scr[h * tp:(h + 1) * tp, h * hd:(h + 1) * hd]


def _sb_sample(proj_main, grp, cache_kt, cache_vt, layer, page_table, sb_bias, db, hd, q_off):
    b, t = grp["b"], grp["t"]
    n_heads = db // hd
    page = cache_kt.shape[-1]
    n_pages = page_table.shape[1]
    tp = SUBLANES
    assert t <= tp
    n_pg = next(g for g in (16, 8, 4, 2, 1) if n_pages % g == 0)
    col = lambda i: lax.slice_in_dim(proj_main, q_off + i * db, q_off + (i + 1) * db, axis=1).reshape(b, t, n_heads, hd)
    by_head = lambda x: jnp.pad(x.transpose(0, 2, 1, 3), ((0, 0), (0, 0), (0, tp - t), (0, 0)))
    bias_b = jnp.broadcast_to(jnp.repeat(sb_bias, tp)[:, None], (n_heads * tp, page))
    tri = (jnp.arange(page)[:, None] > jnp.arange(page)[None, :]).astype(BF16)
    q_spec = pl.BlockSpec((None, n_heads, tp, hd), lambda bb, j, pt: (bb, 0, 0, 0))

    def page_spec(g):
        return pl.BlockSpec((None, None, n_heads, hd, page),
                            lambda bb, j, pt: (layer, pt[bb * n_pages + n_pages - 1 - (j * n_pg + g)], 0, 0, 0))

    const = lambda shape: pl.BlockSpec(shape, lambda bb, j, pt: (0,) * len(shape))
    out = pl.pallas_call(
        functools.partial(_sb_sample_kernel, n_pg, n_heads, tp, page, hd ** -0.5),
        grid_spec=pltpu.PrefetchScalarGridSpec(
            num_scalar_prefetch=1, grid=(b, n_pages // n_pg),
            in_specs=[q_spec] * 3 + [page_spec(g) for g in range(n_pg)] * 2
                     + [const((n_heads * tp, page)), const((page, page))],
            out_specs=q_spec,
            scratch_shapes=[pltpu.VMEM((n_heads * tp, db), BF16),
                            pltpu.VMEM((n_heads, page, hd), F32), pltpu.VMEM((n_heads, page, hd), F32),
                            pltpu.VMEM((n_heads * tp, db), F32), pltpu.VMEM((n_heads * tp, 1), F32)]),
        out_shape=jax.ShapeDtypeStruct((b, n_heads, tp, hd), F32),
        compiler_params=_cparams("arbitrary", "arbitrary"), name="sb_sample",
    )(page_table.reshape(-1), by_head(col(0)), by_head(col(1)), by_head(col(2)),
      *([cache_kt] * n_pg), *([cache_vt] * n_pg), bias_b, tri)
    return out[:, :, :t, :].transpose(0, 2, 1, 3).reshape(b * t, db)


def _merge_kernel(y_ref, r_ref, k_ref, v_ref, og_ref, yb_ref, ga_ref, gb_ref, wa_ref, wb_ref,
                  rk_ref, lg_ref, lb_ref, seg_ref, o_ref, ya_scr):
    @pl.when(pl.program_id(1) == 0)
    def _():
        seg = seg_ref[...]
        y = y_ref[...]
        hd_inv = 1.0 / jnp.sum(seg[0:1, :].astype(F32))
        mu = _head_sum(y, seg) * hd_inv
        yc = y - mu
        var = _head_sum(yc * yc, seg) * hd_inv
        yn = yc * lax.rsqrt(var + GN_EPS) * lg_ref[...] + lb_ref[...]
        bonus = _head_sum(r_ref[...] * k_ref[...] * rk_ref[...], seg)
        ya_scr[...] = ((yn + bonus * v_ref[...]) * og_ref[...]).astype(BF16)
    ma = jnp.dot(ya_scr[...], wa_ref[...], preferred_element_type=F32)
    mb = jnp.dot(yb_ref[...].astype(BF16), wb_ref[...], preferred_element_type=F32)
    o_ref[...] = (_sigmoid(ga_ref[...]) * ma + _sigmoid(gb_ref[...]) * mb).astype(BF16)


def _merge(y, rwkv_r, rwkv_k, rwkv_v, og, y_b, proj_main, gate_off, wa_bf, wb_bf, prm, seg, tm):
    n, da = y.shape
    db = y_b.shape[1]
    d = wa_bf.shape[1]
    tn = _tile(d, 512)
    ga_blk = gate_off // tn
    row = lambda w: pl.BlockSpec((tm, w), lambda i, j: (i, 0))
    vec = lambda w: pl.BlockSpec((1, w), lambda i, j: (0, 0))
    return pl.pallas_call(
        _merge_kernel, grid=(n // tm, d // tn),
        in_specs=[row(da)] * 5 + [row(db),
                  pl.BlockSpec((tm, tn), lambda i, j: (i, ga_blk + j)),
                  pl.BlockSpec((tm, tn), lambda i, j: (i, ga_blk + d // tn + j)),
                  pl.BlockSpec((da, tn), lambda i, j: (0, j)), pl.BlockSpec((db, tn), lambda i, j: (0, j)),
                  vec(da), vec(da), vec(da), pl.BlockSpec((LANES, LANES), lambda i, j: (0, 0))],
        out_specs=pl.BlockSpec((tm, tn), lambda i, j: (i, j)),
        out_shape=jax.ShapeDtypeStruct((n, d), BF16),
        scratch_shapes=[pltpu.VMEM((tm, da), BF16)],
        compiler_params=_cparams("arbitrary", "arbitrary"), name="merge",
    )(y, rwkv_r, rwkv_k, rwkv_v, og, y_b, proj_main, proj_main, wa_bf, wb_bf,
      prm["r_k"], prm["lnx_g"], prm["lnx_b"], seg)


def _wo_ln_kernel(alpha, m_ref, w_ref, x_ref, gm_ref, shf_ref, scf_ref, lg_ref, lb_ref, x1_o, h2_o):
    y = jnp.dot(m_ref[...], w_ref[...], preferred_element_type=F32)
    x1 = _std(alpha * x_ref[...] + gm_ref[...] * y, LN_EPS) * lg_ref[...] + lb_ref[...]
    x1_o[...] = x1
    h2_o[...] = (_std(x1, LN_EPS) * (1.0 + scf_ref[...]) + shf_ref[...]).astype(BF16)


def _wo_ln(merged, x, grp, mod, wo_bf, ln_g, ln_b, alpha, tm):
    n, d = x.shape
    row = pl.BlockSpec((tm, d), lambda i: (i, 0))
    vec = pl.BlockSpec((1, d), lambda i: (0, 0))
    return pl.pallas_call(
        functools.partial(_wo_ln_kernel, alpha), grid=(n // tm,),
        in_specs=[row, pl.BlockSpec((d, d), lambda i: (0, 0)), row,
                  _mod_spec(grp, 2, tm), _mod_spec(grp, 3, tm), _mod_spec(grp, 4, tm), vec, vec],
        out_specs=[row, row],
        out_shape=[jax.ShapeDtypeStruct((n, d), F32), jax.ShapeDtypeStruct((n, d), BF16)],
        compiler_params=_cparams("arbitrary"), name="wo_ln",
    )(merged, wo_bf, x, mod, mod, mod, ln_g, ln_b)


def _peer_q_kernel(ng, hk, h_ref, wq_ref, keys_ref, o_ref):
    qp = jnp.dot(h_ref[...], wq_ref[...], preferred_element_type=F32)
    for g in range(ng):
        o_ref[g] = _dot_nt(keys_ref[g], qp[:, g * hk:(g + 1) * hk].astype(BF16))


def _peer_q(h2, wq_bf, keys_bf, tm):
    n, d = h2.shape
    n_grp, nk, hk = keys_bf.shape
    ng = next(g for g in (8, 4, 2) if n_grp % g == 0)
    return pl.pallas_call(
        functools.partial(_peer_q_kernel, ng, hk), grid=(n // tm, n_grp // ng),
        in_specs=[pl.BlockSpec((tm, d), lambda i, j: (i, 0)),
                  pl.BlockSpec((d, ng * hk), lambda i, j: (0, j)),
                  pl.BlockSpec((ng, nk, hk), lambda i, j: (j, 0, 0))],
        out_specs=pl.BlockSpec((ng, nk, tm), lambda i, j: (j, 0, i)),
        out_shape=jax.ShapeDtypeStruct((n_grp, nk, n), F32),
        compiler_params=_cparams("arbitrary", "arbitrary"), name="peer_q",
    )(h2, wq_bf, keys_bf)


def _top_rows(s, k, payloads=(), rid=None):
    if rid is None:
        rid = lax.broadcasted_iota(jnp.int32, s.shape, 0).astype(F32)
    vals, idxs, picked = [], [], [[] for _ in payloads]
    for _ in range(k):
        m = jnp.max(s, axis=0, keepdims=True)
        idx = jnp.min(jnp.where(s == m, rid, RID_NONE), axis=0, keepdims=True)
        hit = rid == idx
        vals.append(m)
        idxs.append(idx)
        for out, pay in zip(picked, payloads):
            out.append(jnp.max(jnp.where(hit, pay, -1.0), axis=0, keepdims=True))
        s = jnp.where(hit, -jnp.inf, s)
    return vals, idxs, picked


def _peer_topk_kernel(hp, s_ref, g_o, i1_o, i2_o):
    for hh in range(hp):
        rows = slice(hh * TOPK, (hh + 1) * TOPK)
        g_o[rows, :], i1_o[rows, :], i2_o[rows, :] = _peer_topk_head(s_ref[2 * hh], s_ref[2 * hh + 1])


def _peer_topk_head(s0, s1):
    tl = s0.shape[1]
    v0, i0, _ = _top_rows(s0, TOPK)
    v1, i1, _ = _top_rows(s1, TOPK)
    v0c, i0c = jnp.concatenate(v0, axis=0), jnp.concatenate(i0, axis=0)
    v1c, i1c = jnp.concatenate(v1, axis=0), jnp.concatenate(i1, axis=0)
    sub = lax.broadcasted_iota(jnp.int32, (SUBLANES, tl), 0)
    cand, rid, c1, c2 = [], [], [], []
    for i in range(TOPK // 2):
        nj = TOPK // (i + 1)
        for j0 in range(0, nj, SUBLANES):
            jj = sub + j0
            cand.append(jnp.where(jj < nj, v0[i] + v1c[j0:j0 + SUBLANES], -jnp.inf))
            rid.append((jj + i * TOPK).astype(F32))
            c1.append(jnp.broadcast_to(i0[i], (SUBLANES, tl)))
            c2.append(i1c[j0:j0 + SUBLANES])
    for i0_ in range(TOPK // 2, TOPK, SUBLANES):
        cand.append(v0c[i0_:i0_ + SUBLANES] + v1[0])
        rid.append(((sub + i0_) * TOPK).astype(F32))
        c1.append(i0c[i0_:i0_ + SUBLANES])
        c2.append(jnp.broadcast_to(i1[0], (SUBLANES, tl)))
    cat = lambda xs: jnp.concatenate(xs, axis=0)
    best, _, (e1, e2) = _top_rows(cat(cand), TOPK, (cat(c1), cat(c2)), rid=cat(rid))
    best = jnp.concatenate(best, axis=0)
    e = jnp.exp(best - best[0:1])
    return e / jnp.sum(e, axis=0, keepdims=True), jnp.concatenate(e1, axis=0), jnp.concatenate(e2, axis=0)


def _peer_topk(scores_t):
    n_grp, nk, n = scores_t.shape
    n_heads = n_grp // 2
    tl = _tile(n, LANES)
    hp = 4 if n_heads % 4 == 0 else 1
    out = jax.ShapeDtypeStruct((n_heads * TOPK, n), F32)
    ospec = pl.BlockSpec((hp * TOPK, tl), lambda i, h: (h, i))
    return pl.pallas_call(
        functools.partial(_peer_topk_kernel, hp), grid=(n // tl, n_heads // hp),
        in_specs=[pl.BlockSpec((2 * hp, nk, tl), lambda i, h: (h, 0, i))],
        out_specs=[ospec] * 3, out_shape=[out] * 3,
        compiler_params=_cparams("arbitrary", "arbitrary"), name="peer_topk",
    )(scores_t)


def _peer_gate_kernel(tg, nk, g_ref, i1_ref, i2_ref, o_ref, g_s, i1_s, i2_s):
    g_s[...] = g_ref[...].T
    i1_s[...] = i1_ref[...].T
    i2_s[...] = i2_ref[...].T
    m = g_s.shape[1]
    sub = lax.broadcasted_iota(jnp.int32, (nk, m), 0).astype(F32)
    steps = BF16_ROWS

    def body(nb, carry):
        rows = pl.ds(pl.multiple_of(nb * steps, steps), steps)
        g8, a8, b8 = g_s[rows, :], i1_s[rows, :], i2_s[rows, :]
        a_t = [jnp.where(a8[i:i + 1, :] == sub, g8[i:i + 1, :], 0.0).astype(BF16) for i in range(steps)]
        b_t = [jnp.where(b8[i:i + 1, :] == sub, 1.0, 0.0).astype(BF16) for i in range(steps)]
        g_tok = [_dot_nt(a_t[i], b_t[i]) for i in range(steps)]
        by_key = jnp.swapaxes(jnp.stack(g_tok, axis=0), 0, 1)
        for c in range(nk):
            o_ref[rows, c * nk:(c + 1) * nk] = by_key[c].astype(BF16)
        return carry

    lax.fori_loop(0, tg // steps, body, 0)


def _peer_gate(gate_t, i1_t, i2_t, nk):
    m, n = gate_t.shape
    tg = _tile(n, LANES)
    ispec = pl.BlockSpec((m, tg), lambda i: (0, i))
    return pl.pallas_call(
        functools.partial(_peer_gate_kernel, tg, nk), grid=(n // tg,),
        in_specs=[ispec] * 3,
        out_specs=pl.BlockSpec((tg, nk * nk), lambda i: (i, 0)),
        out_shape=jax.ShapeDtypeStruct((n, nk * nk), BF16),
        scratch_shapes=[pltpu.VMEM((tg, m), F32)] * 3,
        compiler_params=_cparams("arbitrary"), name="peer_gate",
    )(gate_t, i1_t, i2_t)


def _gelu_tanh(x):
    return 0.5 * x * (1.0 + jnp.tanh(math.sqrt(2.0 / math.pi) * (x + 0.044715 * (x * x * x))))


def _peer_dense_kernel(alpha, h_ref, g_ref, ut_ref, v_ref, x1_ref, gf_ref, lg_ref, lb_ref, o_ref):
    c = pl.program_id(1)

    @pl.when(c == 0)
    def _():
        o_ref[...] = jnp.zeros_like(o_ref)

    act = _gelu_tanh(jnp.dot(h_ref[...], ut_ref[...], preferred_element_type=F32))
    p = (act * g_ref[...].astype(F32)).astype(BF16)
    o_ref[...] += jnp.dot(p, v_ref[...], preferred_element_type=F32)

    @pl.when(c == pl.num_programs(1) - 1)
    def _():
        z = alpha * x1_ref[...] + gf_ref[...] * o_ref[...]
        o_ref[...] = _std(z, LN_EPS) * lg_ref[...] + lb_ref[...]


def _peer_dense(h2, gmat, ut_all, v_all, layer, x1, grp, mod, ln_g, ln_b, alpha, tm):
    n, d = x1.shape
    e = v_all.shape[1]
    te = _tile(e, 1024)
    row = pl.BlockSpec((tm, d), lambda i, c: (i, 0))
    vec = pl.BlockSpec((1, d), lambda i, c: (0, 0))
    return pl.pallas_call(
        functools.partial(_peer_dense_kernel, alpha), grid=(n // tm, e // te),
        in_specs=[row, pl.BlockSpec((tm, te), lambda i, c: (i, c)),
                  pl.BlockSpec((None, d, te), lambda i, c: (layer, 0, c)),
                  pl.BlockSpec((None, te, d), lambda i, c: (layer, c, 0)),
                  row, _mod_spec(grp, 5, tm), vec, vec],
        out_specs=row, out_shape=jax.ShapeDtypeStruct((n, d), F32),
        compiler_params=_cparams("arbitrary", "arbitrary"), name="peer_dense",
    )(h2, gmat, ut_all, v_all, x1, mod, ln_g, ln_b)


def _run_trunk(x, grp, mods, shift0, wkv0, cache_k, cache_v, page_table, w, dims):
    d, da, db, hd, lp, depth = dims["d"], dims["da"], dims["db"], dims["hd"], dims["lp"], dims["depth"]
    b, t = grp["b"], grp["t"]
    n = b * t
    tm = grp["tm"]
    alpha = (2 * depth) ** 0.25
    n_raw = 3 * da + dims["n_lora"]
    seg, diag = w["seg"], w["diag"]
    k_rows, v_rows, wkv_rows, shift_rows = [], [], [], []
    for l in range(depth):
        mod = mods[l]
        lw = w["layers"][l]
        proj_main = _proj(x, grp, mod, lw["w_main"], _tile(da, 1024))
        proj_lora = _proj(x, grp, mod, lw["w_lora"], lp)
        if grp["per_row"]:
            prev_main = jnp.repeat(shift0[l][:, :3 * da], t, axis=0)
            prev_lora = jnp.repeat(jnp.pad(shift0[l][:, 3 * da:], ((0, 0), (0, lp - dims["n_lora"]))), t, axis=0)
        else:
            prev_main = shift0[l][:, None, :3 * da]
            prev_lora = jnp.pad(shift0[l][:, None, 3 * da:], ((0, 0), (0, 0), (0, lp - dims["n_lora"])))
        r_, w_, k_, v_, a_, b_, og = _rwkv_prep(proj_main, proj_lora, grp, prev_main, prev_lora, lw, seg)
        if t % (LANES // (LANES // hd)) == 0:
            y, s_fin = _wkv_chunked((r_, w_, k_, v_, a_, b_), wkv0[l], grp, hd)
        else:
            y, s_fin = _scan((r_, w_, k_, v_, a_, b_), wkv0[l], grp, seg, diag)
        if cache_k is None:
            y_b = _sb_prompt(proj_main, grp, lw["sb_bias"], db, hd, 3 * da)
        else:
            y_b = _sb_sample(proj_main, grp, cache_k, cache_v, l, page_table, lw["sb_bias"], db, hd, 3 * da)
        merged = _merge(y, r_, k_, v_, og, y_b, proj_main, 3 * da + 3 * db, lw["w_br_a"], lw["w_br_b"], lw, seg,
                        grp["tm"])
        x1, h2 = _wo_ln(merged, x, grp, mod, lw["wo"], lw["ln1_g"], lw["ln1_b"], alpha, grp["tm_prep"])
        scores_t = _peer_q(h2, lw["wq"], lw["keys"], tm)
        gate_t, i1_t, i2_t = _peer_topk(scores_t)
        nk = lw["keys"].shape[1]
        gmat = _peer_gate(gate_t, i1_t, i2_t, nk)
        x = _peer_dense(h2, gmat, w["ut"], w["v"], l, x1, grp, mod, lw["ln2_g"], lw["ln2_b"], alpha, tm)
        p3 = proj_main.reshape(b, t, -1)
        k_rows.append(p3[:, :, 3 * da + db:3 * da + 2 * db].reshape(b, t, db // hd, hd))
        v_rows.append(p3[:, :, 3 * da + 2 * db:3 * da + 3 * db].reshape(b, t, db // hd, hd))
        wkv_rows.append(s_fin)
        shift_rows.append(jnp.concatenate(
            [p3[:, -1, :3 * da], proj_lora.reshape(b, t, lp)[:, -1, :dims["n_lora"]]], axis=-1))
        assert shift_rows[-1].shape[-1] == n_raw
    return (x.reshape(b, t, d), jnp.stack(k_rows), jnp.stack(v_rows), jnp.stack(wkv_rows), jnp.stack(shift_rows))


def kernel(x_prompt, x_sample, cache_k, cache_v, state_wkv, state_shift, page_table, c_prompt, c_sample,
           w_ada, b_ada, w_in, mu_shift, decay_bias, decay_up, aaa_bias, aaa_up, og_up, k_k, k_a, r_k,
           lnx_g, lnx_b, sb_bias, w_branch_a, w_branch_b, w_o, ln1_g, ln1_b, peer_wq, peer_keys, peer_u, peer_v,
           ln2_g, ln2_b):
    depth, d, _ = w_ada.shape
    hd = cache_k.shape[-1]
    da = decay_up.shape[2]
    db = w_branch_b.shape[1]
    wl, al, gl = decay_up.shape[1], aaa_up.shape[1], og_up.shape[1]
    n_lora = wl + al + gl
    lp = -(-n_lora // LANES) * LANES
    assert da % LANES == 0 and db % LANES == 0 and LANES % hd == 0
    dims = dict(d=d, da=da, db=db, hd=hd, lp=lp, depth=depth, n_lora=n_lora)

    seg = (jnp.arange(LANES)[:, None] // hd == jnp.arange(LANES)[None, :] // hd).astype(BF16)
    diag = (jnp.arange(hd)[:, None] == jnp.arange(LANES)[None, :] % hd).astype(F32)
    layers = []
    for l in range(depth):
        rw = 3 * da + n_lora
        pad_rows = lambda m, off: jnp.pad(m, ((off, lp - off - m.shape[0]), (0, 0))).astype(BF16)
        n_grp = peer_keys.shape[1] * peer_keys.shape[2]
        layers.append(dict(
            da=da, lp=lp,
            w_main=jnp.concatenate([w_in[l][:, :3 * da], w_in[l][:, rw:]], axis=1).astype(BF16),
            w_lora=jnp.pad(w_in[l][:, 3 * da:rw], ((0, 0), (0, lp - n_lora))).astype(BF16),
            mu_main=mu_shift[l][None, :3 * da],
            mu_lora=jnp.pad(mu_shift[l][None, 3 * da:], ((0, 0), (0, lp - n_lora))),
            decay_bias=decay_bias[l][None], aaa_bias=aaa_bias[l][None],
            wd=pad_rows(decay_up[l], 0), wa=pad_rows(aaa_up[l], wl), wg=pad_rows(og_up[l], wl + al),
            w_br_a=w_branch_a[l].astype(BF16), w_br_b=w_branch_b[l].astype(BF16),
            k_k=k_k[l][None], k_a=k_a[l][None], r_k=r_k[l].reshape(1, da),
            lnx_g=lnx_g[l][None], lnx_b=lnx_b[l][None], sb_bias=sb_bias[l],
            wo=w_o[l].astype(BF16), ln1_g=ln1_g[l][None], ln1_b=ln1_b[l][None],
            wq=peer_wq[l].astype(BF16), keys=peer_keys[l].reshape(n_grp, *peer_keys.shape[3:]).astype(BF16),
            ln2_g=ln2_g[l][None], ln2_b=ln2_b[l][None]))
    w = dict(seg=seg, diag=diag, layers=layers,
             ut=jnp.swapaxes(peer_u, 1, 2).astype(BF16), v=peer_v.astype(BF16))

    bp, tp_, _ = x_prompt.shape
    bs, ts, _ = x_sample.shape
    mod_all = _ada(jnp.concatenate([c_prompt, c_sample], axis=0), w_ada, b_ada)

    def group(b, t):
        n = b * t
        tm = _tile(n, 512) if t < 512 else _tile(t, 512)
        tm_prep = _tile(n, 256) if t < 256 else _tile(t, 256)
        tm_big = _tile(n, 1024) if t < tm else _tile(t, 1024)
        return dict(b=b, t=t, d=d, tm=tm, tm_prep=tm_prep, tm_big=tm_big, per_row=t < tm)

    gp, gs = group(bp, tp_), group(bs, ts)
    assert gp["per_row"] == (gp["t"] < gp["tm_prep"]) and gs["per_row"] == (gs["t"] < gs["tm_prep"])

    def mods_for(grp, lo, hi):
        out = []
        for l in range(depth):
            m = mod_all[l, lo:hi]
            out.append(jnp.repeat(m, grp["t"], axis=0) if grp["per_row"] else m.reshape(hi - lo, 6, 1, d))
        return out

    shift0_p = jnp.zeros((depth, bp, 3 * da + n_lora), F32)
    wkv0_p = jnp.zeros((depth, bp, da // hd, hd, hd), F32)
    y_p, k_p, v_p, wkv_p, shift_p = _run_trunk(
        x_prompt.reshape(bp * tp_, d), gp, mods_for(gp, 0, bp), shift0_p, wkv0_p, None, None, None, w, dims)
    y_s, k_s, v_s, wkv_s, shift_s = _run_trunk(
        x_sample.reshape(bs * ts, d), gs, mods_for(gs, bp, bp + bs), state_shift, state_wkv,
        cache_k.transpose(0, 1, 3, 4, 2), cache_v.transpose(0, 1, 3, 4, 2), page_table, w, dims)
    return (y_p, y_s, k_p, v_p, wkv_p, shift_p, k_s, v_s, wkv_s, shift_s)
```
